```python
import jax, jax.numpy as jnp
from jax import lax
import numpy as np

D_MODEL = 1024
BATCH = 8
SEQ = 2048
DEPTH = 2
DEC_BATCH = 128
DEC_SEQ = 1
PAST_LEN = 16384
PAGE_SIZE = 128

N_MIXERS = 2
N_LAYERS_A = (DEPTH + 1) // 2
N_LAYERS_B = DEPTH // 2
D_FF = 2816
CONV_A_WIDTH = 31
CONV_B_WIDTH = 3
RMS_EPS = 1e-6
LN_EPS = 1e-5
FFN_RES_WEIGHT = 0.5

kernel_name = "hybrid_conformer_shortconv_decoder_step"


def rms_norm(x, g):
    xf = x.astype(jnp.float32)
    y = xf * lax.rsqrt(jnp.mean(xf * xf, axis=-1, keepdims=True) + RMS_EPS)
    return (y * g.astype(jnp.float32)).astype(x.dtype)


def layer_norm(x, g, b):
    xf = x.astype(jnp.float32)
    mu = jnp.mean(xf, axis=-1, keepdims=True)
    var = jnp.mean(jnp.square(xf - mu), axis=-1, keepdims=True)
    y = (xf - mu) * lax.rsqrt(var + LN_EPS)
    return (y * g.astype(jnp.float32) + b.astype(jnp.float32)).astype(x.dtype)


def swiglu(x, w_gate, w_up, w_down):
    return (jax.nn.silu(x @ w_gate) * (x @ w_up)) @ w_down


def causal_dwconv(u, buf, w):
    width, ch = w.shape
    full = jnp.concatenate([buf.astype(u.dtype), u], axis=1)
    out = lax.conv_general_dilated(
        full, w.astype(u.dtype)[:, None, :], window_strides=(1,), padding="VALID",
        dimension_numbers=("NWC", "WIO", "NWC"), feature_group_count=ch)
    return out, full[:, full.shape[1] - (width - 1):]


def conformer_conv_mixer(x, buf, w_pw1, b_pw1, w_dw, b_dw, ln_g, ln_b, w_pw2, b_pw2):
    h = x @ w_pw1 + b_pw1
    a, g = jnp.split(h, 2, axis=-1)
    v = a * jax.nn.sigmoid(g)
    c, new_buf = causal_dwconv(v, buf, w_dw)
    c = jax.nn.silu(layer_norm(c + b_dw, ln_g, ln_b))
    return c @ w_pw2 + b_pw2, new_buf


def short_gated_conv_mixer(x, buf, w_in, w_conv, w_out):
    bch = x @ w_in
    b_gate, c_gate, h = jnp.split(bch, 3, axis=-1)
    c, new_buf = causal_dwconv(c_gate * h, buf, w_conv)
    return (b_gate * c) @ w_out, new_buf


def run_trunk(x, bufs_a, bufs_b, ffn1_norm, ffn1_w_gate, ffn1_w_up, ffn1_w_down,
              mix_norm, ffn2_norm, ffn2_w_gate, ffn2_w_up, ffn2_w_down, final_norm,
              a_w_pw1, a_b_pw1, a_w_dw, a_b_dw, a_ln_g, a_ln_b, a_w_pw2, a_b_pw2,
              b_w_in, b_w_conv, b_w_out):
    new_a, new_b = [], []
    for i in range(DEPTH):
        x = x + FFN_RES_WEIGHT * swiglu(rms_norm(x, ffn1_norm[i]), ffn1_w_gate[i], ffn1_w_up[i], ffn1_w_down[i])
        hn = rms_norm(x, mix_norm[i])
        j = i // N_MIXERS
        if i % N_MIXERS == 0:
            m, nb = conformer_conv_mixer(hn, bufs_a[j], a_w_pw1[j], a_b_pw1[j], a_w_dw[j], a_b_dw[j],
                                         a_ln_g[j], a_ln_b[j], a_w_pw2[j], a_b_pw2[j])
            new_a.append(nb)
        else:
            m, nb = short_gated_conv_mixer(hn, bufs_b[j], b_w_in[j], b_w_conv[j], b_w_out[j])
            new_b.append(nb)
        x = x + m
        x = x + FFN_RES_WEIGHT * swiglu(rms_norm(x, ffn2_norm[i]), ffn2_w_gate[i], ffn2_w_up[i], ffn2_w_down[i])
    return rms_norm(x, final_norm), jnp.stack(new_a), jnp.stack(new_b)


def setup_inputs(seed: int = 0) -> dict:
    key = jax.random.key(seed)
    ks = iter(jax.random.split(key, 32))
    f32 = jnp.float32
    D = D_MODEL

    def nrm(shape, scale):
        return jax.random.normal(next(ks), shape, f32) * scale

    def gain(shape):
        return 1.0 + 0.02 * jax.random.normal(next(ks), shape, f32)

    return {
        "x_prompt": nrm((BATCH, SEQ, D), 1.0),
        "x_sample": nrm((DEC_BATCH, DEC_SEQ, D), 1.0),
        "state_conv_a": nrm((N_LAYERS_A, DEC_BATCH, CONV_A_WIDTH - 1, D), 1.0),
        "state_conv_b": nrm((N_LAYERS_B, DEC_BATCH, CONV_B_WIDTH - 1, D), 1.0),
        "ffn1_norm": gain((DEPTH, D)),
        "ffn1_w_gate": nrm((DEPTH, D, D_FF), D ** -0.5),
        "ffn1_w_up": nrm((DEPTH, D, D_FF), D ** -0.5),
        "ffn1_w_down": nrm((DEPTH, D_FF, D), D_FF ** -0.5),
        "mix_norm": gain((DEPTH, D)),
        "ffn2_norm": gain((DEPTH, D)),
        "ffn2_w_gate": nrm((DEPTH, D, D_FF), D ** -0.5),
        "ffn2_w_up": nrm((DEPTH, D, D_FF), D ** -0.5),
        "ffn2_w_down": nrm((DEPTH, D_FF, D), D_FF ** -0.5),
        "final_norm": gain((D,)),
        "a_w_pw1": nrm((N_LAYERS_A, D, 2 * D), D ** -0.5),
        "a_b_pw1": nrm((N_LAYERS_A, 2 * D), 0.02),
        "a_w_dw": nrm((N_LAYERS_A, CONV_A_WIDTH, D), CONV_A_WIDTH ** -0.5),
        "a_b_dw": nrm((N_LAYERS_A, D), 0.02),
        "a_ln_g": gain((N_LAYERS_A, D)),
        "a_ln_b": nrm((N_LAYERS_A, D), 0.02),
        "a_w_pw2": nrm((N_LAYERS_A, D, D), D ** -0.5),
        "a_b_pw2": nrm((N_LAYERS_A, D), 0.02),
        "b_w_in": nrm((N_LAYERS_B, D, 3 * D), D ** -0.5),
        "b_w_conv": nrm((N_LAYERS_B, CONV_B_WIDTH, D), CONV_B_WIDTH ** -0.5),
        "b_w_out": nrm((N_LAYERS_B, D, D), D ** -0.5),
    }


def reference(x_prompt, x_sample, state_conv_a, state_conv_b,
              ffn1_norm, ffn1_w_gate, ffn1_w_up, ffn1_w_down,
              mix_norm, ffn2_norm, ffn2_w_gate, ffn2_w_up, ffn2_w_down, final_norm,
              a_w_pw1, a_b_pw1, a_w_dw, a_b_dw, a_ln_g, a_ln_b, a_w_pw2, a_b_pw2,
              b_w_in, b_w_conv, b_w_out):
    weights = (ffn1_norm, ffn1_w_gate, ffn1_w_up, ffn1_w_down,
               mix_norm, ffn2_norm, ffn2_w_gate, ffn2_w_up, ffn2_w_down, final_norm,
               a_w_pw1, a_b_pw1, a_w_dw, a_b_dw, a_ln_g, a_ln_b, a_w_pw2, a_b_pw2,
               b_w_in, b_w_conv, b_w_out)
    zeros_a = jnp.zeros((N_LAYERS_A, x_prompt.shape[0], CONV_A_WIDTH - 1, D_MODEL), x_prompt.dtype)
    zeros_b = jnp.zeros((N_LAYERS_B, x_prompt.shape[0], CONV_B_WIDTH - 1, D_MODEL), x_prompt.dtype)
    y_prompt, new_conv_a_prompt, new_conv_b_prompt = run_trunk(x_prompt, zeros_a, zeros_b, *weights)
    y_sample, new_conv_a_sample, new_conv_b_sample = run_trunk(x_sample, state_conv_a, state_conv_b, *weights)
    return (y_prompt, y_sample, new_conv_a_prompt, new_conv_a_sample, new_conv_b_prompt, new_conv_b_sample)
```

```python
import functools

import jax
import jax.numpy as jnp
from jax.experimental import pallas as pl
from jax.experimental.pallas import tpu as pltpu

RMS_EPS = 1e-6
LN_EPS = 1e-5
FFN_RES_WEIGHT = 0.5

SUBLANES = 8
VMEM_LIMIT_BYTES = 56 * 1024 * 1024

TOKEN_TILE = 512
FF_CHUNK = 512
CONV_ROWS = 32


def _rms(x, g):
    return x * jax.lax.rsqrt(jnp.mean(x * x, axis=-1, keepdims=True) + RMS_EPS) * g


def _dot(a, b):
    return jnp.dot(a, b, preferred_element_type=jnp.float32)


def _resident(shape):
    zeros = (0,) * len(shape)
    return pl.BlockSpec(shape, lambda *_: zeros, pipeline_mode=pl.Buffered(1))


def _params(n_grid):
    return pltpu.CompilerParams(
        dimension_semantics=("arbitrary",) * n_grid,
        vmem_limit_bytes=VMEM_LIMIT_BYTES,
    )


def _ffn_body(x, g_ref, wg_ref, wu_ref, wd_ref):
    d_ff = wg_ref.shape[1]
    h = _rms(x, g_ref[...]).astype(jnp.bfloat16)
    y = None
    for c0 in range(0, d_ff, FF_CHUNK):
        c1 = min(c0 + FF_CHUNK, d_ff)
        gate = _dot(h, wg_ref[:, c0:c1])
        up = _dot(h, wu_ref[:, c0:c1])
        a = (gate * jax.nn.sigmoid(gate) * up).astype(jnp.bfloat16)
        part = _dot(a, wd_ref[c0:c1, :])
        y = part if y is None else y + part
    return x + FFN_RES_WEIGHT * y


def _ffn_kernel(x_ref, g_ref, wg_ref, wu_ref, wd_ref, o_ref):
    o_ref[...] = _ffn_body(x_ref[...], g_ref, wg_ref, wu_ref, wd_ref)


def _ffn_final_kernel(x_ref, g_ref, wg_ref, wu_ref, wd_ref, fg_ref, o_ref):
    o_ref[...] = _rms(_ffn_body(x_ref[...], g_ref, wg_ref, wu_ref, wd_ref), fg_ref[...])


def _ffn(x, g, wg, wu, wd, final_g=None, *, name):
    m, d = x.shape
    tm = min(TOKEN_TILE, m)
    d_ff = wg.shape[1]
    row = pl.BlockSpec((tm, d), lambda i: (i, 0))
    in_specs = [row, _resident((1, d)), _resident((d, d_ff)), _resident((d, d_ff)), _resident((d_ff, d))]
    args = [x, g.reshape(1, d), wg, wu, wd]
    body = _ffn_kernel
    if final_g is not None:
        in_specs.append(_resident((1, d)))
        args.append(final_g.reshape(1, d))
        body = _ffn_final_kernel
    return pl.pallas_call(
        body,
        grid=(m // tm,),
        in_specs=in_specs,
        out_specs=row,
        out_shape=jax.ShapeDtypeStruct((m, d), jnp.float32),
        compiler_params=_params(1),
        name=name,
    )(*args)


def _dwconv_tile(hist_ref, w_ref, out_ref, tm, pad):
    width = w_ref.shape[0]
    first = pad - (width - 1)

    for r0 in range(0, tm, CONV_ROWS):
        acc = hist_ref[r0 + first:r0 + first + CONV_ROWS, :] * w_ref[0:1, :]
        for k in range(1, width):
            acc = acc + hist_ref[r0 + first + k:r0 + first + k + CONV_ROWS, :] * w_ref[k:k + 1, :]
        out_ref[r0:r0 + CONV_ROWS, :] = acc


def _layer_norm(x, g, b):
    mu = jnp.mean(x, axis=-1, keepdims=True)
    xc = x - mu
    var = jnp.mean(xc * xc, axis=-1, keepdims=True)
    return xc * jax.lax.rsqrt(var + LN_EPS) * g + b


def _mixer_a_prompt_kernel(x_ref, gn_ref, w1_ref, b1_ref, wdw_ref, bdw_ref, lng_ref, lnb_ref,
                           w2_ref, b2_ref, o_ref, nb_ref, hist_ref, conv_ref, *, tm, pad):
    t = pl.program_id(1)
    d = x_ref.shape[-1]
    width = wdw_ref.shape[0]

    @pl.when(t == 0)
    def _():
        hist_ref[0:pad, :] = jnp.zeros((pad, d), jnp.float32)

    x = x_ref[0]
    hn = _rms(x, gn_ref[...]).astype(jnp.bfloat16)
    a = _dot(hn, w1_ref[:, 0:d]) + b1_ref[:, 0:d]
    g = _dot(hn, w1_ref[:, d:2 * d]) + b1_ref[:, d:2 * d]
    hist_ref[pad:pad + tm, :] = a * jax.nn.sigmoid(g)

    _dwconv_tile(hist_ref, wdw_ref, conv_ref, tm, pad)

    c = _layer_norm(conv_ref[...] + bdw_ref[...], lng_ref[...], lnb_ref[...])
    c = (c * jax.nn.sigmoid(c)).astype(jnp.bfloat16)
    o_ref[0] = x + _dot(c, w2_ref[...]) + b2_ref[...]

    @pl.when(t == pl.num_programs(1) - 1)
    def _():
        nb_ref[0, 0] = hist_ref[pad + tm - (width - 1):pad + tm, :]

    hist_ref[0:pad, :] = hist_ref[tm:tm + pad, :]


def _mixer_a_prompt(x, gn, w1, b1, wdw, bdw, lng, lnb, w2, b2):
    bsz, seq, d = x.shape
    width = wdw.shape[0]
    tm = min(TOKEN_TILE, seq)
    pad = -(-(width - 1) // SUBLANES) * SUBLANES
    row = pl.BlockSpec((1, tm, d), lambda b, t: (b, t, 0))
    vec = lambda v: v.reshape(1, -1)
    out, new_buf = pl.pallas_call(
        functools.partial(_mixer_a_prompt_kernel, tm=tm, pad=pad),
        grid=(bsz, seq // tm),
        in_specs=[row, _resident((1, d)), _resident((d, 2 * d)), _resident((1, 2 * d)),
                  _resident((width, d)), _resident((1, d)), _resident((1, d)), _resident((1, d)),
                  _resident((d, d)), _resident((1, d))],
        out_specs=[row, pl.BlockSpec((1, 1, width - 1, d), lambda b, t: (0, b, 0, 0))],
        out_shape=[jax.ShapeDtypeStruct((bsz, seq, d), jnp.float32),
                   jax.ShapeDtypeStruct((1, bsz, width - 1, d), jnp.float32)],
        scratch_shapes=[pltpu.VMEM((pad + tm, d), jnp.float32), pltpu.VMEM((tm, d), jnp.float32)],
        compiler_params=_params(2),
        name="mixer_a_prompt",
    )(x, vec(gn), w1, vec(b1), wdw, vec(bdw), vec(lng), vec(lnb), w2, vec(b2))
    return out, new_buf


def _mixer_b_prompt_kernel(x_ref, gn_ref, win_ref, wc_ref, wout_ref, o_ref, nb_ref, hist_ref,
                           conv_ref, *, tm, pad):
    t = pl.program_id(1)
    d = x_ref.shape[-1]
    width = wc_ref.shape[0]

    @pl.when(t == 0)
    def _():
        hist_ref[0:pad, :] = jnp.zeros((pad, d), jnp.float32)

    x = x_ref[0]
    hn = _rms(x, gn_ref[...]).astype(jnp.bfloat16)
    b_gate = _dot(hn, win_ref[:, 0:d])
    c_gate = _dot(hn, win_ref[:, d:2 * d])
    h = _dot(hn, win_ref[:, 2 * d:3 * d])
    hist_ref[pad:pad + tm, :] = c_gate * h

    _dwconv_tile(hist_ref, wc_ref, conv_ref, tm, pad)

    y = (b_gate * conv_ref[...]).astype(jnp.bfloat16)
    o_ref[0] = x + _dot(y, wout_ref[...])

    @pl.when(t == pl.num_programs(1) - 1)
    def _():
        nb_ref[0, 0] = hist_ref[pad + tm - (width - 1):pad + tm, :]

    hist_ref[0:pad, :] = hist_ref[tm:tm + pad, :]


def _mixer_b_prompt(x, gn, win, wc, wout):
    bsz, seq, d = x.shape
    width = wc.shape[0]
    tm = min(TOKEN_TILE, seq)
    pad = -(-(width - 1) // SUBLANES) * SUBLANES
    row = pl.BlockSpec((1, tm, d), lambda b, t: (b, t, 0))
    out, new_buf = pl.pallas_call(
        functools.partial(_mixer_b_prompt_kernel, tm=tm, pad=pad),
        grid=(bsz, seq // tm),
        in_specs=[row, _resident((1, d)), _resident((d, 3 * d)), _resident((width, d)),
                  _resident((d, d))],
        out_specs=[row, pl.BlockSpec((1, 1, width - 1, d), lambda b, t: (0, b, 0, 0))],
        out_shape=[jax.ShapeDtypeStruct((bsz, seq, d), jnp.float32),
                   jax.ShapeDtypeStruct((1, bsz, width - 1, d), jnp.float32)],
        scratch_shapes=[pltpu.VMEM((pad + tm, d), jnp.float32), pltpu.VMEM((tm, d), jnp.float32)],
        compiler_params=_params(2),
        name="mixer_b_prompt",
    )(x, gn.reshape(1, d), win, wc, wout)
    return out, new_buf


def _mixer_a_sample_kernel(x_ref, st_ref, gn_ref, w1_ref, b1_ref, wdw_ref, bdw_ref, lng_ref,
                           lnb_ref, w2_ref, b2_ref, o_ref, ns_ref, acc_ref):
    k = pl.program_id(0)
    last = pl.num_programs(0) - 1
    d = x_ref.shape[-1]

    @pl.when(k == 0)
    def _():
        acc_ref[...] = jnp.zeros_like(acc_ref)

    @pl.when(k < last)
    def _():
        row = st_ref[...]
        acc_ref[...] += row * wdw_ref[pl.ds(k, 1), :]
        ns_ref[...] = row

    @pl.when(k == last)
    def _():
        x = x_ref[...]
        hn = _rms(x, gn_ref[...]).astype(jnp.bfloat16)
        a = _dot(hn, w1_ref[:, 0:d]) + b1_ref[:, 0:d]
        g = _dot(hn, w1_ref[:, d:2 * d]) + b1_ref[:, d:2 * d]
        v = a * jax.nn.sigmoid(g)
        ns_ref[...] = v
        conv = acc_ref[...] + v * wdw_ref[pl.ds(k, 1), :]
        c = _layer_norm(conv + bdw_ref[...], lng_ref[...], lnb_ref[...])
        c = (c * jax.nn.sigmoid(c)).astype(jnp.bfloat16)
        o_ref[...] = x + _dot(c, w2_ref[...]) + b2_ref[...]


def _state_specs(n, hist, d):
    st_in = pl.BlockSpec((n, d), lambda k: (0, jnp.minimum(k, hist - 1)))
    st_out = pl.BlockSpec((n, d), lambda k: (0, jnp.maximum(k - 1, 0)))
    return st_in, st_out


def _mixer_a_sample(x, state, gn, w1, b1, wdw, bdw, lng, lnb, w2, b2):
    n, d = x.shape
    width = wdw.shape[0]
    st_in, st_out = _state_specs(n, width - 1, d)
    vec = lambda v: v.reshape(1, -1)
    out, new_state = pl.pallas_call(
        _mixer_a_sample_kernel,
        grid=(width,),
        in_specs=[_resident((n, d)), st_in, _resident((1, d)), _resident((d, 2 * d)),
                  _resident((1, 2 * d)), _resident((width, d)), _resident((1, d)),
                  _resident((1, d)), _resident((1, d)), _resident((d, d)), _resident((1, d))],
        out_specs=[pl.BlockSpec((n, d), lambda k: (0, 0)), st_out],
        out_shape=[jax.ShapeDtypeStruct((n, d), jnp.float32),
                   jax.ShapeDtypeStruct((n, (width - 1) * d), jnp.float32)],
        scratch_shapes=[pltpu.VMEM((n, d), jnp.float32)],
        compiler_params=_params(1),
        name="mixer_a_sample",
    )(x, state.reshape(n, (width - 1) * d), vec(gn), w1, vec(b1), wdw, vec(bdw), vec(lng),
      vec(lnb), w2, vec(b2))
    return out, new_state.reshape(1, n, width - 1, d)


def _mixer_b_sample_kernel(x_ref, st_ref, gn_ref, win_ref, wc_ref, wout_ref, o_ref, ns_ref, acc_ref):
    k = pl.program_id(0)
    last = pl.num_programs(0) - 1
    d = x_ref.shape[-1]

    @pl.when(k == 0)
    def _():
        acc_ref[...] = jnp.zeros_like(acc_ref)

    @pl.when(k < last)
    def _():
        row = st_ref[...]
        acc_ref[...] += row * wc_ref[pl.ds(k, 1), :]
        ns_ref[...] = row

    @pl.when(k == last)
    def _():
        x = x_ref[...]
        hn = _rms(x, gn_ref[...]).astype(jnp.bfloat16)
        b_gate = _dot(hn, win_ref[:, 0:d])
        c_gate = _dot(hn, win_ref[:, d:2 * d])
        h = _dot(hn, win_ref[:, 2 * d:3 * d])
        u = c_gate * h
        ns_ref[...] = u
        conv = acc_ref[...] + u * wc_ref[pl.ds(k, 1), :]
        y = (b_gate * conv).astype(jnp.bfloat16)
        o_ref[...] = x + _dot(y, wout_ref[...])


def _mixer_b_sample(x, state, gn, win, wc, wout):
    n, d = x.shape
    width = wc.shape[0]
    st_in, st_out = _state_specs(n, width - 1, d)
    out, new_state = pl.pallas_call(
        _mixer_b_sample_kernel,
        grid=(width,),
        in_specs=[_resident((n, d)), st_in, _resident((1, d)), _resident((d, 3 * d)),
                  _resident((width, d)), _resident((d, d))],
        out_specs=[pl.BlockSpec((n, d), lambda k: (0, 0)), st_out],
        out_shape=[jax.ShapeDtypeStruct((n, d), jnp.float32),
                   jax.ShapeDtypeStruct((n, (width - 1) * d), jnp.float32)],
        scratch_shapes=[pltpu.VMEM((n, d), jnp.float32)],
        compiler_params=_params(1),
        name="mixer_b_sample",
    )(x, state.reshape(n, (width - 1) * d), gn.reshape(1, d), win, wc, wout)
    return out, new_state.reshape(1, n, width - 1, d)


def _cat(per_layer):
    return per_layer[0] if len(per_layer) == 1 else jnp.concatenate(per_layer, axis=0)


def kernel(x_prompt, x_sample, state_conv_a, state_conv_b, ffn1_norm, ffn1_w_gate, ffn1_w_up, ffn1_w_down, mix_norm, ffn2_norm, ffn2_w_gate, ffn2_w_up, ffn2_w_down, final_norm, a_w_pw1, a_b_pw1, a_w_dw, a_b_dw, a_ln_g, a_ln_b, a_w_pw2, a_b_pw2, b_w_in, b_w_conv, b_w_out):
    bf = lambda w: w.astype(jnp.bfloat16)
    depth = ffn1_norm.shape[0]
    bsz, seq, d = x_prompt.shape
    n_dec = x_sample.shape[0]

    xp = x_prompt.reshape(bsz * seq, d)
    xs = x_sample.reshape(n_dec, d)
    new_a_p, new_a_s, new_b_p, new_b_s = [], [], [], []
    for i in range(depth):
        j = i // 2
        w_ffn1 = (ffn1_norm[i], bf(ffn1_w_gate[i]), bf(ffn1_w_up[i]), bf(ffn1_w_down[i]))
        w_ffn2 = (ffn2_norm[i], bf(ffn2_w_gate[i]), bf(ffn2_w_up[i]), bf(ffn2_w_down[i]))
        xp = _ffn(xp, *w_ffn1, name=f"ffn1_prompt_{i}")
        xs = _ffn(xs, *w_ffn1, name=f"ffn1_sample_{i}")
        if i % 2 == 0:
            w_mix = (mix_norm[i], bf(a_w_pw1[j]), a_b_pw1[j], a_w_dw[j], a_b_dw[j], a_ln_g[j],
                     a_ln_b[j], bf(a_w_pw2[j]), a_b_pw2[j])
            xp3, nb = _mixer_a_prompt(xp.reshape(bsz, seq, d), *w_mix)
            xs, ns = _mixer_a_sample(xs, state_conv_a[j], *w_mix)
            new_a_p.append(nb)
            new_a_s.append(ns)
        else:
            w_mix = (mix_norm[i], bf(b_w_in[j]), b_w_conv[j], bf(b_w_out[j]))
            xp3, nb = _mixer_b_prompt(xp.reshape(bsz, seq, d), *w_mix)
            xs, ns = _mixer_b_sample(xs, state_conv_b[j], *w_mix)
            new_b_p.append(nb)
            new_b_s.append(ns)
        xp = xp3.reshape(bsz * seq, d)
        fin = final_norm if i == depth - 1 else None
        xp = _ffn(xp, *w_ffn2, fin, name=f"ffn2_prompt_{i}")
        xs = _ffn(xs, *w_ffn2, fin, name=f"ffn2_sample_{i}")

    return (xp.reshape(bsz, seq, d), xs.reshape(n_dec, 1, d),
            _cat(new_a_p), _cat(new_a_s), _cat(new_b_p), _cat(new_b_s))
```

```python
import functools

import jax
import jax.numpy as jnp
from jax.experimental import pallas as pl
from jax.experimental.pallas import tpu as pltpu

RMS_EPS = 1e-6
LN_EPS = 1e-5
FFN_RES_WEIGHT = 0.5

SUBLANES = 8
LANES = 128
BF16_ROWS = 16
VMEM_LIMIT_BYTES = 56 * 1024 * 1024

TOKEN_TILE = 512
FF_CHUNK = 512
CONV_ROWS = 64
CAST_STEPS = 16


def _rms(x, g):
    return x * jax.lax.rsqrt(jnp.mean(x * x, axis=-1, keepdims=True) + RMS_EPS) * g


def _layer_norm(x, g, b):
    mu = jnp.mean(x, axis=-1, keepdims=True)
    xc = x - mu
    var = jnp.mean(xc * xc, axis=-1, keepdims=True)
    return xc * jax.lax.rsqrt(var + LN_EPS) * g + b


def _dot(a, b):
    return jnp.dot(a, b, preferred_element_type=jnp.float32)


def _resident(shape):
    zeros = (0,) * len(shape)
    return pl.BlockSpec(shape, lambda *_: zeros, pipeline_mode=pl.Buffered(1))


def _layer_weight(layer, rows, cols):
    return pl.BlockSpec((None, rows, cols), lambda *_: (layer, 0, 0), pipeline_mode=pl.Buffered(1))


def _params(n_grid):
    return pltpu.CompilerParams(
        dimension_semantics=("arbitrary",) * n_grid,
        vmem_limit_bytes=VMEM_LIMIT_BYTES,
    )


def _cast_kernel(*refs):
    n = len(refs) // 2
    for src, dst in zip(refs[:n], refs[n:]):
        dst[...] = src[...].astype(jnp.bfloat16)


def _cast_weights(*ws):
    specs = []
    for w in ws:
        n_layers, rows, cols = w.shape
        assert rows % (CAST_STEPS * BF16_ROWS) == 0
        specs.append(pl.BlockSpec((n_layers, rows // CAST_STEPS, cols), lambda i: (0, i, 0)))
    return pl.pallas_call(
        _cast_kernel,
        grid=(CAST_STEPS,),
        in_specs=specs,
        out_specs=specs,
        out_shape=[jax.ShapeDtypeStruct(w.shape, jnp.bfloat16) for w in ws],
        compiler_params=_params(1),
        name="cast_weights",
    )(*ws)


def _ffn_body(x, g, wg_ref, wu_ref, wd_ref):
    d_ff = wg_ref.shape[1]
    h = _rms(x, g).astype(jnp.bfloat16)
    y = None
    for c0 in range(0, d_ff, FF_CHUNK):
        c1 = min(c0 + FF_CHUNK, d_ff)
        gate = _dot(h, wg_ref[:, c0:c1])
        up = _dot(h, wu_ref[:, c0:c1])
        a = (gate * jax.nn.sigmoid(gate) * up).astype(jnp.bfloat16)
        part = _dot(a, wd_ref[c0:c1, :])
        y = part if y is None else y + part
    return x + FFN_RES_WEIGHT * y


def _ffn_kernel(x_ref, vec_ref, wg_ref, wu_ref, wd_ref, o_ref, *, g_row, final_row):
    out = _ffn_body(x_ref[...], vec_ref[g_row:g_row + 1, :], wg_ref, wu_ref, wd_ref)
    if final_row is not None:
        out = _rms(out, vec_ref[final_row:final_row + 1, :])
    o_ref[...] = out


def _ffn(x, vecs, wg, wu, wd, *, layer, g_row, final_row=None, name):
    m, d = x.shape
    tm = min(TOKEN_TILE, m)
    d_ff = wg.shape[2]
    row = pl.BlockSpec((tm, d), lambda i: (i, 0))
    return pl.pallas_call(
        functools.partial(_ffn_kernel, g_row=g_row, final_row=final_row),
        grid=(m // tm,),
        in_specs=[row, _resident(vecs.shape), _layer_weight(layer, d, d_ff),
                  _layer_weight(layer, d, d_ff), _layer_weight(layer, d_ff, d)],
        out_specs=row,
        out_shape=jax.ShapeDtypeStruct((m, d), jnp.float32),
        compiler_params=_params(1),
        name=name,
    )(x, vecs, wg, wu, wd)


def _dwconv_tile(hist_ref, w_ref, w_row, width, out_ref, tm, pad):
    first = pad - (width - 1)
    d = out_ref.shape[1]
    g = CONV_ROWS // SUBLANES
    nq = (first + width - 1) // SUBLANES + 1
    sub = jax.lax.broadcasted_iota(jnp.int32, (g, SUBLANES, LANES), 1)
    for c0 in range(0, d, LANES):
        taps = [jnp.broadcast_to(w_ref[w_row + k:w_row + k + 1, c0:c0 + LANES], (SUBLANES, LANES))
                for k in range(width)]
        for r0 in range(0, tm, CONV_ROWS):
            n_rows = CONV_ROWS + SUBLANES * (nq - 1)
            x = hist_ref[r0:r0 + n_rows, c0:c0 + LANES].reshape(g + nq - 1, SUBLANES, LANES)
            acc = None
            for r in range(SUBLANES):
                n = g if r == 0 else g + 1
                y = None
                for q in range(nq):
                    k = SUBLANES * q + r - first
                    if 0 <= k < width:
                        term = x[q:q + n] * taps[k][None]
                        y = term if y is None else y + term
                if y is None:
                    continue
                if r == 0:
                    z = y
                else:
                    rot = pltpu.roll(y, SUBLANES - r, axis=1)
                    z = jnp.where(sub < SUBLANES - r, rot[0:g], rot[1:g + 1])
                acc = z if acc is None else acc + z
            out_ref[r0:r0 + CONV_ROWS, c0:c0 + LANES] = acc.reshape(CONV_ROWS, LANES)


def _conv_pad(width):
    return -(-(width - 1) // SUBLANES) * SUBLANES


def _mixer_a_prompt_kernel(x_ref, vec_ref, w1_ref, w2_ref, o_ref, nb_ref, hist_ref, conv_ref,
                           *, tm, pad, width, v0):
    t = pl.program_id(1)
    d = x_ref.shape[-1]
    vec = lambda i: vec_ref[v0 + i:v0 + i + 1, :]

    @pl.when(t == 0)
    def _():
        hist_ref[0:pad, :] = jnp.zeros((pad, d), jnp.float32)

    x = x_ref[0]
    hn = _rms(x, vec(0)).astype(jnp.bfloat16)
    a = _dot(hn, w1_ref[:, 0:d]) + vec(1)
    g = _dot(hn, w1_ref[:, d:2 * d]) + vec(2)
    hist_ref[pad:pad + tm, :] = a * jax.nn.sigmoid(g)

    _dwconv_tile(hist_ref, vec_ref, v0 + 7, width, conv_ref, tm, pad)

    c = _layer_norm(conv_ref[...] + vec(3), vec(4), vec(5))
    c = (c * jax.nn.sigmoid(c)).astype(jnp.bfloat16)
    o_ref[0] = x + _dot(c, w2_ref[...]) + vec(6)

    @pl.when(t == pl.num_programs(1) - 1)
    def _():
        nb_ref[0, 0] = hist_ref[pad + tm - (width - 1):pad + tm, :]

    hist_ref[0:pad, :] = hist_ref[tm:tm + pad, :]


def _mixer_a_prompt(x, vecs, w1, w2, *, layer, width, v0):
    bsz, seq, d = x.shape
    tm = min(TOKEN_TILE, seq)
    pad = _conv_pad(width)
    row = pl.BlockSpec((1, tm, d), lambda b, t: (b, t, 0))
    return pl.pallas_call(
        functools.partial(_mixer_a_prompt_kernel, tm=tm, pad=pad, width=width, v0=v0),
        grid=(bsz, seq // tm),
        in_specs=[row, _resident(vecs.shape), _layer_weight(layer, d, 2 * d),
                  _layer_weight(layer, d, d)],
        out_specs=[row, pl.BlockSpec((1, 1, width - 1, d), lambda b, t: (0, b, 0, 0))],
        out_shape=[jax.ShapeDtypeStruct((bsz, seq, d), jnp.float32),
                   jax.ShapeDtypeStruct((1, bsz, width - 1, d), jnp.float32)],
        scratch_shapes=[pltpu.VMEM((pad + tm, d), jnp.float32), pltpu.VMEM((tm, d), jnp.float32)],
        compiler_params=_params(2),
        name="mixer_a_prompt",
    )(x, vecs, w1, w2)


def _mixer_b_prompt_kernel(x_ref, vec_ref, win_ref, wout_ref, o_ref, nb_ref, hist_ref, conv_ref,
                           *, tm, pad, width, v0):
    t = pl.program_id(1)
    d = x_ref.shape[-1]

    @pl.when(t == 0)
    def _():
        hist_ref[0:pad, :] = jnp.zeros((pad, d), jnp.float32)

    x = x_ref[0]
    hn = _rms(x, vec_ref[v0:v0 + 1, :]).astype(jnp.bfloat16)
    b_gate = _dot(hn, win_ref[:, 0:d])
    c_gate = _dot(hn, win_ref[:, d:2 * d])
    h = _dot(hn, win_ref[:, 2 * d:3 * d])
    hist_ref[pad:pad + tm, :] = c_gate * h

    _dwconv_tile(hist_ref, vec_ref, v0 + 1, width, conv_ref, tm, pad)

    y = (b_gate * conv_ref[...]).astype(jnp.bfloat16)
    o_ref[0] = x + _dot(y, wout_ref[...])

    @pl.when(t == pl.num_programs(1) - 1)
    def _():
        nb_ref[0, 0] = hist_ref[pad + tm - (width - 1):pad + tm, :]

    hist_ref[0:pad, :] = hist_ref[tm:tm + pad, :]


def _mixer_b_prompt(x, vecs, win, wout, *, layer, width, v0):
    bsz, seq, d = x.shape
    tm = min(TOKEN_TILE, seq)
    pad = _conv_pad(width)
    row = pl.BlockSpec((1, tm, d), lambda b, t: (b, t, 0))
    return pl.pallas_call(
        functools.partial(_mixer_b_prompt_kernel, tm=tm, pad=pad, width=width, v0=v0),
        grid=(bsz, seq // tm),
        in_specs=[row, _resident(vecs.shape), _layer_weight(layer, d, 3 * d),
                  _layer_weight(layer, d, d)],
        out_specs=[row, pl.BlockSpec((1, 1, width - 1, d), lambda b, t: (0, b, 0, 0))],
        out_shape=[jax.ShapeDtypeStruct((bsz, seq, d), jnp.float32),
                   jax.ShapeDtypeStruct((1, bsz, width - 1, d), jnp.float32)],
        scratch_shapes=[pltpu.VMEM((pad + tm, d), jnp.float32), pltpu.VMEM((tm, d), jnp.float32)],
        compiler_params=_params(2),
        name="mixer_b_prompt",
    )(x, vecs, win, wout)


def _state_specs(n, hist, d):
    st_in = pl.BlockSpec((None, n, d), lambda k: (jnp.minimum(k, hist - 1), 0, 0))
    st_out = pl.BlockSpec((None, n, d), lambda k: (jnp.maximum(k - 1, 0), 0, 0))
    return st_in, st_out


def _mixer_a_sample_kernel(x_ref, st_ref, vec_ref, w1_ref, w2_ref, o_ref, ns_ref, acc_ref, *, v0):
    k = pl.program_id(0)
    last = pl.num_programs(0) - 1
    d = x_ref.shape[-1]
    vec = lambda i: vec_ref[v0 + i:v0 + i + 1, :]
    tap = vec_ref[pl.ds(v0 + 7 + k, 1), :]

    @pl.when(k == 0)
    def _():
        acc_ref[...] = jnp.zeros_like(acc_ref)

    @pl.when(k < last)
    def _():
        row = st_ref[...]
        acc_ref[...] += row * tap
        ns_ref[...] = row

    @pl.when(k == last)
    def _():
        x = x_ref[...]
        hn = _rms(x, vec(0)).astype(jnp.bfloat16)
        a = _dot(hn, w1_ref[:, 0:d]) + vec(1)
        g = _dot(hn, w1_ref[:, d:2 * d]) + vec(2)
        v = a * jax.nn.sigmoid(g)
        ns_ref[...] = v
        c = _layer_norm(acc_ref[...] + v * tap + vec(3), vec(4), vec(5))
        c = (c * jax.nn.sigmoid(c)).astype(jnp.bfloat16)
        o_ref[...] = x + _dot(c, w2_ref[...]) + vec(6)


def _mixer_a_sample(x, state, vecs, w1, w2, *, layer, width, v0):
    n, d = x.shape
    st_in, st_out = _state_specs(n, width - 1, d)
    return pl.pallas_call(
        functools.partial(_mixer_a_sample_kernel, v0=v0),
        grid=(width,),
        in_specs=[_resident((n, d)), st_in, _resident(vecs.shape), _layer_weight(layer, d, 2 * d),
                  _layer_weight(layer, d, d)],
        out_specs=[pl.BlockSpec((n, d), lambda k: (0, 0)), st_out],
        out_shape=[jax.ShapeDtypeStruct((n, d), jnp.float32),
                   jax.ShapeDtypeStruct((width - 1, n, d), jnp.float32)],
        scratch_shapes=[pltpu.VMEM((n, d), jnp.float32)],
        compiler_params=_params(1),
        name="mixer_a_sample",
    )(x, state, vecs, w1, w2)


def _mixer_b_sample_kernel(x_ref, st_ref, vec_ref, win_ref, wout_ref, o_ref, ns_ref, acc_ref, *, v0):
    k = pl.program_id(0)
    last = pl.num_programs(0) - 1
    d = x_ref.shape[-1]
    tap = vec_ref[pl.ds(v0 + 1 + k, 1), :]

    @pl.when(k == 0)
    def _():
        acc_ref[...] = jnp.zeros_like(acc_ref)

    @pl.when(k < last)
    def _():
        row = st_ref[...]
        acc_ref[...] += row * tap
        ns_ref[...] = row

    @pl.when(k == last)
    def _():
        x = x_ref[...]
        hn = _rms(x, vec_ref[v0:v0 + 1, :]).astype(jnp.bfloat16)
        b_gate = _dot(hn, win_ref[:, 0:d])
        c_gate = _dot(hn, win_ref[:, d:2 * d])
        h = _dot(hn, win_ref[:, 2 * d:3 * d])
        u = c_gate * h
        ns_ref[...] = u
        y = (b_gate * (acc_ref[...] + u * tap)).astype(jnp.bfloat16)
        o_ref[...] = x + _dot(y, wout_ref[...])


def _mixer_b_sample(x, state, vecs, win, wout, *, layer, width, v0):
    n, d = x.shape
    st_in, st_out = _state_specs(n, width - 1, d)
    return pl.pallas_call(
        functools.partial(_mixer_b_sample_kernel, v0=v0),
        grid=(width,),
        in_specs=[_resident((n, d)), st_in, _resident(vecs.shape), _layer_weight(layer, d, 3 * d),
                  _layer_weight(layer, d, d)],
        out_specs=[pl.BlockSpec((n, d), lambda k: (0, 0)), st_out],
        out_shape=[jax.ShapeDtypeStruct((n, d), jnp.float32),
                   jax.ShapeDtypeStruct((width - 1, n, d), jnp.float32)],
        scratch_shapes=[pltpu.VMEM((n, d), jnp.float32)],
        compiler_params=_params(1),
        name="mixer_b_sample",
    )(x, state, vecs, win, wout)


def _pack_vectors(d, groups):
    rows, starts, n = [], [], 0
    for v in groups:
        v = v.reshape(-1, d)
        starts.append(n)
        rows.append(v)
        n += v.shape[0]
    fill = -n % SUBLANES
    if fill:
        rows.append(jnp.zeros((fill, d), jnp.float32))
    return jnp.concatenate(rows, axis=0), starts


def _cat(per_layer):
    return per_layer[0] if len(per_layer) == 1 else jnp.concatenate(per_layer, axis=0)


def kernel(x_prompt, x_sample, state_conv_a, state_conv_b, ffn1_norm, ffn1_w_gate, ffn1_w_up, ffn1_w_down, mix_norm, ffn2_norm, ffn2_w_gate, ffn2_w_up, ffn2_w_down, final_norm, a_w_pw1, a_b_pw1, a_w_dw, a_b_dw, a_ln_g, a_ln_b, a_w_pw2, a_b_pw2, b_w_in, b_w_conv, b_w_out):
    depth = ffn1_norm.shape[0]
    bsz, seq, d = x_prompt.shape
    n_dec = x_sample.shape[0]
    width_a = a_w_dw.shape[1]
    width_b = b_w_conv.shape[1]

    (w1g, w1u, w1d, w2g, w2u, w2d, a_w1, a_w2, b_win, b_wout) = _cast_weights(
        ffn1_w_gate, ffn1_w_up, ffn1_w_down, ffn2_w_gate, ffn2_w_up, ffn2_w_down,
        a_w_pw1, a_w_pw2, b_w_in, b_w_out)

    groups = [ffn1_norm, ffn2_norm, final_norm]
    for i in range(depth):
        j = i // 2
        if i % 2 == 0:
            groups += [mix_norm[i], a_b_pw1[j], a_b_dw[j], a_ln_g[j], a_ln_b[j], a_b_pw2[j], a_w_dw[j]]
        else:
            groups += [mix_norm[i], b_w_conv[j]]
    vecs, starts = _pack_vectors(d, groups)
    ffn1_row, ffn2_row, final_row = starts[0], starts[1], starts[2]
    mix_rows, p = [], 3
    for i in range(depth):
        mix_rows.append(starts[p])
        p += 7 if i % 2 == 0 else 2

    xp = x_prompt.reshape(bsz * seq, d)
    xs = x_sample.reshape(n_dec, d)
    new_a_p, new_a_s, new_b_p, new_b_s = [], [], [], []
    for i in range(depth):
        j = i // 2
        xp = _ffn(xp, vecs, w1g, w1u, w1d, layer=i, g_row=ffn1_row + i, name=f"ffn1_prompt_{i}")
        xs = _ffn(xs, vecs, w1g, w1u, w1d, layer=i, g_row=ffn1_row + i, name=f"ffn1_sample_{i}")
        if i % 2 == 0:
            kw = dict(layer=j, width=width_a, v0=mix_rows[i])
            xp3, nb = _mixer_a_prompt(xp.reshape(bsz, seq, d), vecs, a_w1, a_w2, **kw)
            state = jnp.transpose(state_conv_a[j], (1, 0, 2))
            xs, ns = _mixer_a_sample(xs, state, vecs, a_w1, a_w2, **kw)
            new_a_p.append(nb)
            new_a_s.append(jnp.transpose(ns, (1, 0, 2))[None])
        else:
            kw = dict(layer=j, width=width_b, v0=mix_rows[i])
            xp3, nb = _mixer_b_prompt(xp.reshape(bsz, seq, d), vecs, b_win, b_wout, **kw)
            state = jnp.transpose(state_conv_b[j], (1, 0, 2))
            xs, ns = _mixer_b_sample(xs, state, vecs, b_win, b_wout, **kw)
            new_b_p.append(nb)
            new_b_s.append(jnp.transpose(ns, (1, 0, 2))[None])
        xp = xp3.reshape(bsz * seq, d)
        fin = final_row if i == depth - 1 else None
        xp = _ffn(xp, vecs, w2g, w2u, w2d, layer=i, g_row=ffn2_row + i, final_row=fin,
                  name=f"ffn2_prompt_{i}")
        xs = _ffn(xs, vecs, w2g, w2u, w2d, layer=i, g_row=ffn2_row + i, final_row=fin,
                  name=f"ffn2_sample_{i}")

    return (xp.reshape(bsz, seq, d), xs.reshape(n_dec, 1, d),
            _cat(new_a_p), _cat(new_a_s), _cat(new_b_p), _cat(new_b_s))
```

```python
import functools

import jax
import jax.numpy as jnp
from jax.experimental import pallas as pl
from jax.experimental.pallas import tpu as pltpu

RMS_EPS = 1e-6
LN_EPS = 1e-5
FFN_RES_WEIGHT = 0.5

SUBLANES = 8
LANES = 128
BF16_ROWS = 16
VMEM_LIMIT_BYTES = 56 * 1024 * 1024

TOKEN_TILE = 512
FF_CHUNK = 512
CONV_ROWS = 64
CAST_STEPS = 16


def _rms(x, g):
    return x * jax.lax.rsqrt(jnp.mean(x * x, axis=-1, keepdims=True) + RMS_EPS) * g


def _layer_norm(x, g, b):
    mu = jnp.mean(x, axis=-1, keepdims=True)
    xc = x - mu
    var = jnp.mean(xc * xc, axis=-1, keepdims=True)
    return xc * jax.lax.rsqrt(var + LN_EPS) * g + b


def _dot(a, b):
    return jnp.dot(a, b, preferred_element_type=jnp.float32)


def _resident(shape):
    zeros = (0,) * len(shape)
    return pl.BlockSpec(shape, lambda *_: zeros, pipeline_mode=pl.Buffered(1))


def _layer_weight(layer, rows, cols):
    return pl.BlockSpec((None, rows, cols), lambda *_: (layer, 0, 0), pipeline_mode=pl.Buffered(1))


def _params(n_grid):
    return pltpu.CompilerParams(
        dimension_semantics=("arbitrary",) * n_grid,
        vmem_limit_bytes=VMEM_LIMIT_BYTES,
    )


def _cast_kernel(*refs):
    n = len(refs) // 2
    for src, dst in zip(refs[:n], refs[n:]):
        dst[...] = src[...].astype(jnp.bfloat16)


def _cast_weights(*ws):
    specs = []
    for w in ws:
        n_layers, rows, cols = w.shape
        assert rows % (CAST_STEPS * BF16_ROWS) == 0
        specs.append(pl.BlockSpec((n_layers, rows // CAST_STEPS, cols), lambda i: (0, i, 0)))
    return pl.pallas_call(
        _cast_kernel,
        grid=(CAST_STEPS,),
        in_specs=specs,
        out_specs=specs,
        out_shape=[jax.ShapeDtypeStruct(w.shape, jnp.bfloat16) for w in ws],
        compiler_params=_params(1),
        name="cast_weights",
    )(*ws)


def _ffn_body(x, g, wg_ref, wu_ref, wd_ref):
    d_ff = wg_ref.shape[1]
    h = _rms(x, g).astype(jnp.bfloat16)
    y = None
    for c0 in range(0, d_ff, FF_CHUNK):
        c1 = min(c0 + FF_CHUNK, d_ff)
        gate = _dot(h, wg_ref[:, c0:c1])
        up = _dot(h, wu_ref[:, c0:c1])
        a = (gate * jax.nn.sigmoid(gate) * up).astype(jnp.bfloat16)
        part = _dot(a, wd_ref[c0:c1, :])
        y = part if y is None else y + part
    return x + FFN_RES_WEIGHT * y


def _ffn_kernel(x_ref, vec_ref, wg_ref, wu_ref, wd_ref, o_ref, *, g_row, final_row):
    out = _ffn_body(x_ref[...], vec_ref[g_row:g_row + 1, :], wg_ref, wu_ref, wd_ref)
    if final_row is not None:
        out = _rms(out, vec_ref[final_row:final_row + 1, :])
    o_ref[...] = out


def _ffn(x, vecs, wg, wu, wd, *, layer, g_row, final_row=None, name):
    m, d = x.shape
    tm = min(TOKEN_TILE, m)
    d_ff = wg.shape[2]
    row = pl.BlockSpec((tm, d), lambda i: (i, 0))
    return pl.pallas_call(
        functools.partial(_ffn_kernel, g_row=g_row, final_row=final_row),
        grid=(m // tm,),
        in_specs=[row, _resident(vecs.shape), _layer_weight(layer, d, d_ff),
                  _layer_weight(layer, d, d_ff), _layer_weight(layer, d_ff, d)],
        out_specs=row,
        out_shape=jax.ShapeDtypeStruct((m, d), jnp.float32),
        compiler_params=_params(1),
        name=name,
    )(x, vecs, wg, wu, wd)


def _dwconv_tile(hist_ref, w_ref, w_row, width, out_ref, tm, pad):
    first = pad - (width - 1)
    d = out_ref.shape[1]
    g = CONV_ROWS // SUBLANES
    nq = (first + width - 1) // SUBLANES + 1
    sub = jax.lax.broadcasted_iota(jnp.int32, (g, SUBLANES, LANES), 1)
    for c0 in range(0, d, LANES):
        taps = [jnp.broadcast_to(w_ref[w_row + k:w_row + k + 1, c0:c0 + LANES], (SUBLANES, LANES))
                for k in range(width)]
        for r0 in range(0, tm, CONV_ROWS):
            n_rows = CONV_ROWS + SUBLANES * (nq - 1)
            x = hist_ref[r0:r0 + n_rows, c0:c0 + LANES].reshape(g + nq - 1, SUBLANES, LANES)
            acc = None
            for r in range(SUBLANES):
                n = g if r == 0 else g + 1
                y = None
                for q in range(nq):
                    k = SUBLANES * q + r - first
                    if 0 <= k < width:
                        term = x[q:q + n] * taps[k][None]
                        y = term if y is None else y + term
                if y is None:
                    continue
                if r == 0:
                    z = y
                else:
                    rot = pltpu.roll(y, SUBLANES - r, axis=1)
                    z = jnp.where(sub < SUBLANES - r, rot[0:g], rot[1:g + 1])
                acc = z if acc is None else acc + z
            out_ref[r0:r0 + CONV_ROWS, c0:c0 + LANES] = acc.reshape(CONV_ROWS, LANES)


def _conv_pad(width):
    return -(-(width - 1) // SUBLANES) * SUBLANES


def _ffn_glu_kernel(x_ref, vec_ref, wg_ref, wu_ref, wd_ref, w1_ref, x1_ref, v_ref, nb_ref,
                    *, g_row, v0, width, tiles_per_seq):
    s = pl.program_id(0)
    tm, d = x_ref.shape
    vec = lambda i: vec_ref[v0 + i:v0 + i + 1, :]
    x1 = _ffn_body(x_ref[...], vec_ref[g_row:g_row + 1, :], wg_ref, wu_ref, wd_ref)
    x1_ref[...] = x1
    hn = _rms(x1, vec(0)).astype(jnp.bfloat16)
    a = _dot(hn, w1_ref[:, 0:d]) + vec(1)
    g = _dot(hn, w1_ref[:, d:2 * d]) + vec(2)
    v_ref[...] = a * jax.nn.sigmoid(g)

    @pl.when(s % tiles_per_seq == tiles_per_seq - 1)
    def _():
        nb_ref[0, 0] = v_ref[tm - (width - 1):tm, :]


def _ffn_glu(x, vecs, wg, wu, wd, w1, *, layer, mix_layer, g_row, v0, width, seq):
    m, d = x.shape
    tm = min(TOKEN_TILE, seq)
    tiles_per_seq = seq // tm
    d_ff = wg.shape[2]
    row = pl.BlockSpec((tm, d), lambda s: (s, 0))
    return pl.pallas_call(
        functools.partial(_ffn_glu_kernel, g_row=g_row, v0=v0, width=width,
                          tiles_per_seq=tiles_per_seq),
        grid=(m // tm,),
        in_specs=[row, _resident(vecs.shape), _layer_weight(layer, d, d_ff),
                  _layer_weight(layer, d, d_ff), _layer_weight(layer, d_ff, d),
                  _layer_weight(mix_layer, d, 2 * d)],
        out_specs=[row, row,
                   pl.BlockSpec((1, 1, width - 1, d), lambda s: (0, s // tiles_per_seq, 0, 0))],
        out_shape=[jax.ShapeDtypeStruct((m, d), jnp.float32),
                   jax.ShapeDtypeStruct((m, d), jnp.float32),
                   jax.ShapeDtypeStruct((1, m // seq, width - 1, d), jnp.float32)],
        compiler_params=_params(1),
        name="ffn1_glu_prompt",
    )(x, vecs, wg, wu, wd, w1)


def _bits(x):
    return jax.lax.bitcast_convert_type(x, jnp.uint32)


def _zero_after(tiles, xlu_hops):
    b = _bits(tiles)
    t = b[0]
    for i in range(1, b.shape[0]):
        t = t | b[i]
    t = (t >> 16) >> 16
    for _ in range(xlu_hops):
        t = pltpu.roll(t, 1, axis=1)
    return t


def _after(x, zero):
    if zero is None:
        return x
    return jnp.where(zero == 0, x, 0.0)


CHAIN_TILES = 8
CHAIN_XLU_HOPS = 1
LN_ROWS = 64


def _dwconv_stream(tail_ref, keep_tail, v_ref, w_ref, w_row, width, out_ref, *, chained):
    tm, d = v_ref.shape
    pad = tail_ref.shape[0]
    first = pad - (width - 1)
    g = CHAIN_TILES
    nq = (first + width - 1) // SUBLANES + 1
    assert pad == SUBLANES * (nq - 1) and (tm // SUBLANES) % g == 0
    sub = jax.lax.broadcasted_iota(jnp.int32, (g, SUBLANES, LANES), 1)
    taps_of = [[(q, SUBLANES * q + r - first) for q in range(nq)
                if 0 <= SUBLANES * q + r - first < width] for r in range(SUBLANES)]
    zero = None
    for c0 in range(0, d, LANES):
        lanes = slice(c0, c0 + LANES)
        w = [jnp.broadcast_to(w_ref[w_row + k:w_row + k + 1, lanes], (SUBLANES, LANES))
             for k in range(width)]
        win = tail_ref[:, lanes]
        if keep_tail is not None:
            win = jnp.where(keep_tail, win, 0.0)
        win = win.reshape(nq - 1, SUBLANES, LANES)
        halo = None
        for j0 in range(0, tm // SUBLANES, g):
            new = v_ref[j0 * SUBLANES:(j0 + g) * SUBLANES, lanes].reshape(g, SUBLANES, LANES)
            if chained:
                new = jnp.concatenate([new[:g - 1], _after(new[g - 1], zero)[None]], axis=0)
            x = jnp.concatenate([win, new], axis=0)
            acc, new_halo = None, []
            for r in range(SUBLANES):
                if not taps_of[r]:
                    continue
                lo = 0 if (r == 0 or halo is None) else 1
                n = g if r == 0 else g + 1
                y = None
                for q, k in taps_of[r]:
                    term = x[q + lo:q + n] * w[k][None]
                    y = term if y is None else y + term
                if r == 0:
                    z = y
                else:
                    rot = pltpu.roll(y, SUBLANES - r, axis=1)
                    if halo is not None:
                        rot = jnp.concatenate([halo[len(new_halo)], rot], axis=0)
                    new_halo.append(rot[g:g + 1])
                    z = jnp.where(sub < SUBLANES - r, rot[0:g], rot[1:g + 1])
                acc = z if acc is None else acc + z
            out_ref[j0 * SUBLANES:(j0 + g) * SUBLANES, lanes] = acc.reshape(g * SUBLANES, LANES)
            halo = new_halo
            win = x[g:g + nq - 1]
            if chained:
                zero = _zero_after(acc, CHAIN_XLU_HOPS)
    return zero


def _conv_ln_swish(tail_ref, keep_tail, v_ref, vec_ref, v0, width, conv_ref, c_ref, *, chained):
    tm, d = v_ref.shape
    pad = tail_ref.shape[0]
    vec = lambda i: vec_ref[v0 + i:v0 + i + 1, :]
    zero = _dwconv_stream(tail_ref, keep_tail, v_ref, vec_ref, v0 + 7, width, conv_ref,
                          chained=chained)
    for r0 in range(0, tm, LN_ROWS):
        cin = conv_ref[r0:r0 + LN_ROWS, :]
        if chained:
            corner = _after(cin[0:SUBLANES, 0:LANES], zero)
            top = jnp.concatenate([corner, cin[0:SUBLANES, LANES:]], axis=1)
            cin = jnp.concatenate([top, cin[SUBLANES:]], axis=0)
        c = _layer_norm(cin + vec(3), vec(4), vec(5))
        c = c * jax.nn.sigmoid(c)
        c_ref[r0:r0 + LN_ROWS, :] = c.astype(jnp.bfloat16)
        if chained:
            zero = _zero_after(c.reshape(LN_ROWS // SUBLANES, SUBLANES, d)[:, :, 0:LANES], 0)
    tail_ref[...] = v_ref[tm - pad:tm, :]


def _conv_ffn_kernel(x1_ref, v_ref, vec_ref, w2_ref, wg_ref, wu_ref, wd_ref, o_ref,
                     c_buf, tail_ref, conv_ref, *, g_row, final_row, v0, width, tiles_per_seq):
    s = pl.program_id(0)
    slot = s % 2
    stage = functools.partial(_conv_ln_swish, tail_ref, v_ref=v_ref, vec_ref=vec_ref, v0=v0,
                              width=width, conv_ref=conv_ref)

    @pl.when(s == 0)
    def _():
        tail_ref[...] = jnp.zeros_like(tail_ref)
        stage(keep_tail=None, c_ref=c_buf.at[0], chained=False)

    @pl.when(s > 0)
    def _():
        c = c_buf[1 - slot]
        x2 = x1_ref[...] + _dot(c, w2_ref[...]) + vec_ref[v0 + 6:v0 + 7, :]
        out = _ffn_body(x2, vec_ref[g_row:g_row + 1, :], wg_ref, wu_ref, wd_ref)
        if final_row is not None:
            out = _rms(out, vec_ref[final_row:final_row + 1, :])
        o_ref[...] = out
        stage(keep_tail=s % tiles_per_seq != 0, c_ref=c_buf.at[slot], chained=True)


def _conv_ffn(x1, v, vecs, w2, wg, wu, wd, *, layer, mix_layer, g_row, final_row, v0, width, seq):
    m, d = x1.shape
    tm = min(TOKEN_TILE, seq)
    n_tiles = m // tm
    d_ff = wg.shape[2]
    pad = _conv_pad(width)
    behind = pl.BlockSpec((tm, d), lambda s: (jnp.maximum(s - 1, 0), 0))
    ahead = pl.BlockSpec((tm, d), lambda s: (jnp.minimum(s, n_tiles - 1), 0))
    return pl.pallas_call(
        functools.partial(_conv_ffn_kernel, g_row=g_row, final_row=final_row, v0=v0, width=width,
                          tiles_per_seq=seq // tm),
        grid=(n_tiles + 1,),
        in_specs=[behind, ahead, _resident(vecs.shape), _layer_weight(mix_layer, d, d),
                  _layer_weight(layer, d, d_ff), _layer_weight(layer, d, d_ff),
                  _layer_weight(layer, d_ff, d)],
        out_specs=behind,
        out_shape=jax.ShapeDtypeStruct((m, d), jnp.float32),
        scratch_shapes=[pltpu.VMEM((2, tm, d), jnp.bfloat16), pltpu.VMEM((pad, d), jnp.float32),
                        pltpu.VMEM((tm, d), jnp.float32)],
        compiler_params=_params(1),
        name="conv_ffn2_prompt",
    )(x1, v, vecs, w2, wg, wu, wd)


def _mixer_b_prompt_kernel(x_ref, vec_ref, win_ref, wout_ref, o_ref, nb_ref, hist_ref, conv_ref,
                           *, tm, pad, width, v0):
    t = pl.program_id(1)
    d = x_ref.shape[-1]

    @pl.when(t == 0)
    def _():
        hist_ref[0:pad, :] = jnp.zeros((pad, d), jnp.float32)

    x = x_ref[0]
    hn = _rms(x, vec_ref[v0:v0 + 1, :]).astype(jnp.bfloat16)
    b_gate = _dot(hn, win_ref[:, 0:d])
    c_gate = _dot(hn, win_ref[:, d:2 * d])
    h = _dot(hn, win_ref[:, 2 * d:3 * d])
    hist_ref[pad:pad + tm, :] = c_gate * h

    _dwconv_tile(hist_ref, vec_ref, v0 + 1, width, conv_ref, tm, pad)

    y = (b_gate * conv_ref[...]).astype(jnp.bfloat16)
    o_ref[0] = x + _dot(y, wout_ref[...])

    @pl.when(t == pl.num_programs(1) - 1)
    def _():
        nb_ref[0, 0] = hist_ref[pad + tm - (width - 1):pad + tm, :]

    hist_ref[0:pad, :] = hist_ref[tm:tm + pad, :]


def _mixer_b_prompt(x, vecs, win, wout, *, layer, width, v0):
    bsz, seq, d = x.shape
    tm = min(TOKEN_TILE, seq)
    pad = _conv_pad(width)
    row = pl.BlockSpec((1, tm, d), lambda b, t: (b, t, 0))
    return pl.pallas_call(
        functools.partial(_mixer_b_prompt_kernel, tm=tm, pad=pad, width=width, v0=v0),
        grid=(bsz, seq // tm),
        in_specs=[row, _resident(vecs.shape), _layer_weight(layer, d, 3 * d),
                  _layer_weight(layer, d, d)],
        out_specs=[row, pl.BlockSpec((1, 1, width - 1, d), lambda b, t: (0, b, 0, 0))],
        out_shape=[jax.ShapeDtypeStruct((bsz, seq, d), jnp.float32),
                   jax.ShapeDtypeStruct((1, bsz, width - 1, d), jnp.float32)],
        scratch_shapes=[pltpu.VMEM((pad + tm, d), jnp.float32), pltpu.VMEM((tm, d), jnp.float32)],
        compiler_params=_params(2),
        name="mixer_b_prompt",
    )(x, vecs, win, wout)


def _state_specs(n, hist, d):
    st_in = pl.BlockSpec((None, n, d), lambda k: (jnp.minimum(k, hist - 1), 0, 0))
    st_out = pl.BlockSpec((None, n, d), lambda k: (jnp.maximum(k - 1, 0), 0, 0))
    return st_in, st_out


def _mixer_a_sample_kernel(x_ref, st_ref, vec_ref, w1_ref, w2_ref, o_ref, ns_ref, acc_ref, *, v0):
    k = pl.program_id(0)
    last = pl.num_programs(0) - 1
    d = x_ref.shape[-1]
    vec = lambda i: vec_ref[v0 + i:v0 + i + 1, :]
    tap = vec_ref[pl.ds(v0 + 7 + k, 1), :]

    @pl.when(k == 0)
    def _():
        acc_ref[...] = jnp.zeros_like(acc_ref)

    @pl.when(k < last)
    def _():
        row = st_ref[...]
        acc_ref[...] += row * tap
        ns_ref[...] = row

    @pl.when(k == last)
    def _():
        x = x_ref[...]
        hn = _rms(x, vec(0)).astype(jnp.bfloat16)
        a = _dot(hn, w1_ref[:, 0:d]) + vec(1)
        g = _dot(hn, w1_ref[:, d:2 * d]) + vec(2)
        v = a * jax.nn.sigmoid(g)
        ns_ref[...] = v
        c = _layer_norm(acc_ref[...] + v * tap + vec(3), vec(4), vec(5))
        c = (c * jax.nn.sigmoid(c)).astype(jnp.bfloat16)
        o_ref[...] = x + _dot(c, w2_ref[...]) + vec(6)


def _mixer_a_sample(x, state, vecs, w1, w2, *, layer, width, v0):
    n, d = x.shape
    st_in, st_out = _state_specs(n, width - 1, d)
    return pl.pallas_call(
        functools.partial(_mixer_a_sample_kernel, v0=v0),
        grid=(width,),
        in_specs=[_resident((n, d)), st_in, _resident(vecs.shape), _layer_weight(layer, d, 2 * d),
                  _layer_weight(layer, d, d)],
        out_specs=[pl.BlockSpec((n, d), lambda k: (0, 0)), st_out],
        out_shape=[jax.ShapeDtypeStruct((n, d), jnp.float32),
                   jax.ShapeDtypeStruct((width - 1, n, d), jnp.float32)],
        scratch_shapes=[pltpu.VMEM((n, d), jnp.float32)],
        compiler_params=_params(1),
        name="mixer_a_sample",
    )(x, state, vecs, w1, w2)


def _mixer_b_sample_kernel(x_ref, st_ref, vec_ref, win_ref, wout_ref, o_ref, ns_ref, acc_ref, *, v0):
    k = pl.program_id(0)
    last = pl.num_programs(0) - 1
    d = x_ref.shape[-1]
    tap = vec_ref[pl.ds(v0 + 1 + k, 1), :]

    @pl.when(k == 0)
    def _():
        acc_ref[...] = jnp.zeros_like(acc_ref)

    @pl.when(k < last)
    def _():
        row = st_ref[...]
        acc_ref[...] += row * tap
        ns_ref[...] = row

    @pl.when(k == last)
    def _():
        x = x_ref[...]
        hn = _rms(x, vec_ref[v0:v0 + 1, :]).astype(jnp.bfloat16)
        b_gate = _dot(hn, win_ref[:, 0:d])
        c_gate = _dot(hn, win_ref[:, d:2 * d])
        h = _dot(hn, win_ref[:, 2 * d:3 * d])
        u = c_gate * h
        ns_ref[...] = u
        y = (b_gate * (acc_ref[...] + u * tap)).astype(jnp.bfloat16)
        o_ref[...] = x + _dot(y, wout_ref[...])


def _mixer_b_sample(x, state, vecs, win, wout, *, layer, width, v0):
    n, d = x.shape
    st_in, st_out = _state_specs(n, width - 1, d)
    return pl.pallas_call(
        functools.partial(_mixer_b_sample_kernel, v0=v0),
        grid=(width,),
        in_specs=[_resident((n, d)), st_in, _resident(vecs.shape), _layer_weight(layer, d, 3 * d),
                  _layer_weight(layer, d, d)],
        out_specs=[pl.BlockSpec((n, d), lambda k: (0, 0)), st_out],
        out_shape=[jax.ShapeDtypeStruct((n, d), jnp.float32),
                   jax.ShapeDtypeStruct((width - 1, n, d), jnp.float32)],
        scratch_shapes=[pltpu.VMEM((n, d), jnp.float32)],
        compiler_params=_params(1),
        name="mixer_b_sample",
    )(x, state, vecs, win, wout)


def _pack_vectors(d, groups):
    rows, starts, n = [], [], 0
    for v in groups:
        v = v.reshape(-1, d)
        starts.append(n)
        rows.append(v)
        n += v.shape[0]
    fill = -n % SUBLANES
    if fill:
        rows.append(jnp.zeros((fill, d), jnp.float32))
    return jnp.concatenate(rows, axis=0), starts


def _cat(per_layer):
    return per_layer[0] if len(per_layer) == 1 else jnp.concatenate(per_layer, axis=0)


def kernel(x_prompt, x_sample, state_conv_a, state_conv_b, ffn1_norm, ffn1_w_gate, ffn1_w_up, ffn1_w_down, mix_norm, ffn2_norm, ffn2_w_gate, ffn2_w_up, ffn2_w_down, final_norm, a_w_pw1, a_b_pw1, a_w_dw, a_b_dw, a_ln_g, a_ln_b, a_w_pw2, a_b_pw2, b_w_in, b_w_conv, b_w_out):
    depth = ffn1_norm.shape[0]
    bsz, seq, d = x_prompt.shape
    n_dec = x_sample.shape[0]
    width_a = a_w_dw.shape[1]
    width_b = b_w_conv.shape[1]

    (w1g, w1u, w1d, w2g, w2u, w2d, a_w1, a_w2, b_win, b_wout) = _cast_weights(
        ffn1_w_gate, ffn1_w_up, ffn1_w_down, ffn2_w_gate, ffn2_w_up, ffn2_w_down,
        a_w_pw1, a_w_pw2, b_w_in, b_w_out)

    groups = [ffn1_norm, ffn2_norm, final_norm]
    for i in range(depth):
        j = i // 2
        if i % 2 == 0:
            groups += [mix_norm[i], a_b_pw1[j], a_b_dw[j], a_ln_g[j], a_ln_b[j], a_b_pw2[j], a_w_dw[j]]
        else:
            groups += [mix_norm[i], b_w_conv[j]]
    vecs, starts = _pack_vectors(d, groups)
    ffn1_row, ffn2_row, final_row = starts[0], starts[1], starts[2]
    mix_rows, p = [], 3
    for i in range(depth):
        mix_rows.append(starts[p])
        p += 7 if i % 2 == 0 else 2

    xp = x_prompt.reshape(bsz * seq, d)
    xs = x_sample.reshape(n_dec, d)
    new_a_p, new_a_s, new_b_p, new_b_s = [], [], [], []
    for i in range(depth):
        j = i // 2
        fin = final_row if i == depth - 1 else None
        ffn1 = dict(layer=i, g_row=ffn1_row + i)
        ffn2 = dict(layer=i, g_row=ffn2_row + i, final_row=fin)
        xs = _ffn(xs, vecs, w1g, w1u, w1d, name=f"ffn1_sample_{i}", **ffn1)
        if i % 2 == 0:
            kw = dict(mix_layer=j, width=width_a, v0=mix_rows[i], seq=seq)
            x1, v, nb = _ffn_glu(xp, vecs, w1g, w1u, w1d, a_w1, **ffn1, **kw)
            xp = _conv_ffn(x1, v, vecs, a_w2, w2g, w2u, w2d, **ffn2, **kw)
            state = jnp.transpose(state_conv_a[j], (1, 0, 2))
            xs, ns = _mixer_a_sample(xs, state, vecs, a_w1, a_w2, layer=j, width=width_a,
                                     v0=mix_rows[i])
            new_a_p.append(nb)
            new_a_s.append(jnp.transpose(ns, (1, 0, 2))[None])
        else:
            kw = dict(layer=j, width=width_b, v0=mix_rows[i])
            xp = _ffn(xp, vecs, w1g, w1u, w1d, name=f"ffn1_prompt_{i}", **ffn1)
            xp3, nb = _mixer_b_prompt(xp.reshape(bsz, seq, d), vecs, b_win, b_wout, **kw)
            xp = _ffn(xp3.reshape(bsz * seq, d), vecs, w2g, w2u, w2d, name=f"ffn2_prompt_{i}",
                      **ffn2)
            state = jnp.transpose(state_conv_b[j], (1, 0, 2))
            xs, ns = _mixer_b_sample(xs, state, vecs, b_win, b_wout, **kw)
            new_b_p.append(nb)
            new_b_s.append(jnp.transpose(ns, (1, 0, 2))[None])
        xs = _ffn(xs, vecs, w2g, w2u, w2d, name=f"ffn2_sample_{i}", **ffn2)

    return (xp.reshape(bsz, seq, d), xs.reshape(n_dec, 1, d),
            _cat(new_a_p), _cat(new_a_s), _cat(new_b_p), _cat(new_b_s))
```

```python
import functools

import jax
import jax.numpy as jnp
from jax.experimental import pallas as pl
from jax.experimental.pallas import tpu as pltpu

RMS_EPS = 1e-6
LN_EPS = 1e-5
FFN_RES_WEIGHT = 0.5

SUBLANES = 8
LANES = 128
BF16_ROWS = 16
VMEM_LIMIT_BYTES = 56 * 1024 * 1024

TOKEN_TILE = 512
FF_CHUNK = 512
CAST_STEPS = 16


def _rms(x, g):
    return x * jax.lax.rsqrt(jnp.mean(x * x, axis=-1, keepdims=True) + RMS_EPS) * g


def _layer_norm(x, g, b):
    mu = jnp.mean(x, axis=-1, keepdims=True)
    xc = x - mu
    var = jnp.mean(xc * xc, axis=-1, keepdims=True)
    return xc * jax.lax.rsqrt(var + LN_EPS) * g + b


def _dot(a, b):
    return jnp.dot(a, b, preferred_element_type=jnp.float32)


def _resident(shape):
    zeros = (0,) * len(shape)
    return pl.BlockSpec(shape, lambda *_: zeros, pipeline_mode=pl.Buffered(1))


def _params(n_grid):
    return pltpu.CompilerParams(
        dimension_semantics=("arbitrary",) * n_grid,
        vmem_limit_bytes=VMEM_LIMIT_BYTES,
    )


def _cast_kernel(*refs):
    n = len(refs) // 2
    for src, dst in zip(refs[:n], refs[n:]):
        dst[...] = src[...].astype(jnp.bfloat16)


def _cast_weights(items):
    in_specs, out_specs, out_shape = [], [], []
    for w, layer in items:
        _, rows, cols = w.shape
        assert rows % (CAST_STEPS * BF16_ROWS) == 0
        blk = rows // CAST_STEPS
        in_specs.append(pl.BlockSpec((None, blk, cols), lambda i, layer=layer: (layer, i, 0)))
        out_specs.append(pl.BlockSpec((blk, cols), lambda i: (i, 0)))
        out_shape.append(jax.ShapeDtypeStruct((rows, cols), jnp.bfloat16))
    return pl.pallas_call(
        _cast_kernel,
        grid=(CAST_STEPS,),
        in_specs=in_specs,
        out_specs=out_specs,
        out_shape=out_shape,
        compiler_params=_params(1),
        name="cast_weights",
    )(*[w for w, _ in items])


def _side_cast_specs(items, n_tiles, tile_of_step):
    in_specs, out_specs, out_shape = [], [], []
    for w, layer in items:
        _, rows, cols = w.shape
        n_blocks = n_tiles
        while rows % (n_blocks * BF16_ROWS) or n_tiles % n_blocks:
            n_blocks -= 1
        blk, per = rows // n_blocks, n_tiles // n_blocks
        in_specs.append(pl.BlockSpec(
            (None, blk, cols), lambda s, layer=layer, per=per: (layer, tile_of_step(s) // per, 0)))
        out_specs.append(pl.BlockSpec((blk, cols), lambda s, per=per: (tile_of_step(s) // per, 0)))
        out_shape.append(jax.ShapeDtypeStruct((rows, cols), jnp.bfloat16))
    return in_specs, out_specs, out_shape


def _side_cast(srcs, dsts):
    for src, dst in zip(srcs, dsts):
        dst[...] = src[...].astype(jnp.bfloat16)


def _ffn_body(x, g, wg_ref, wu_ref, wd_ref):
    d_ff = wg_ref.shape[1]
    h = _rms(x, g).astype(jnp.bfloat16)
    y = None
    for c0 in range(0, d_ff, FF_CHUNK):
        c1 = min(c0 + FF_CHUNK, d_ff)
        gate = _dot(h, wg_ref[:, c0:c1])
        up = _dot(h, wu_ref[:, c0:c1])
        a = (gate * jax.nn.sigmoid(gate) * up).astype(jnp.bfloat16)
        part = _dot(a, wd_ref[c0:c1, :])
        y = part if y is None else y + part
    return x + FFN_RES_WEIGHT * y


def _ffn_out(x, vec_ref, wg_ref, wu_ref, wd_ref, g_row, final_row):
    out = _ffn_body(x, vec_ref[g_row:g_row + 1, :], wg_ref, wu_ref, wd_ref)
    if final_row is not None:
        out = _rms(out, vec_ref[final_row:final_row + 1, :])
    return out


def _ffn_kernel(x_ref, vec_ref, wg_ref, wu_ref, wd_ref, xs_ref, *rest, g_row, final_row, n_tiles):
    n_cast = (len(rest) - 2) // 2
    srcs, (o_ref, ys_ref), dsts = rest[:n_cast], rest[n_cast:n_cast + 2], rest[n_cast + 2:]
    s = pl.program_id(0)
    ffn = functools.partial(_ffn_out, vec_ref=vec_ref, wg_ref=wg_ref, wu_ref=wu_ref, wd_ref=wd_ref,
                            g_row=g_row, final_row=final_row)

    @pl.when(s < n_tiles)
    def _():
        o_ref[...] = ffn(x_ref[...])
        _side_cast(srcs, dsts)

    @pl.when(s == n_tiles)
    def _():
        ys_ref[...] = ffn(xs_ref[...])


def _ffn(x, xs, vecs, wg, wu, wd, *, g_row, final_row=None, cast=(), name):
    m, d = x.shape
    tm = min(TOKEN_TILE, m)
    n_tiles = m // tm
    tile = lambda s: jnp.minimum(s, n_tiles - 1)
    row = pl.BlockSpec((tm, d), lambda s: (tile(s), 0))
    c_in, c_out, c_shape = _side_cast_specs(cast, n_tiles, tile)
    res = pl.pallas_call(
        functools.partial(_ffn_kernel, g_row=g_row, final_row=final_row, n_tiles=n_tiles),
        grid=(n_tiles + 1,),
        in_specs=[row, _resident(vecs.shape), _resident(wg.shape), _resident(wu.shape),
                  _resident(wd.shape), _resident(xs.shape)] + c_in,
        out_specs=[row, _resident(xs.shape)] + c_out,
        out_shape=[jax.ShapeDtypeStruct((m, d), jnp.float32),
                   jax.ShapeDtypeStruct(xs.shape, jnp.float32)] + c_shape,
        compiler_params=_params(1),
        name=name,
    )(x, vecs, wg, wu, wd, xs, *[w for w, _ in cast])
    return res[0], res[1], res[2:]


CHAIN_TILES = 8
CHAIN_XLU_HOPS = 1
LN_ROWS = 64


def _zero_after(tiles, xlu_hops):
    b = jax.lax.bitcast_convert_type(tiles, jnp.uint32)
    t = b[0]
    for i in range(1, b.shape[0]):
        t = t | b[i]
    t = (t >> 16) >> 16
    for _ in range(xlu_hops):
        t = pltpu.roll(t, 1, axis=1)
    return t


def _after(x, zero):
    if zero is None:
        return x
    return jnp.where(zero == 0, x, 0.0)


def _conv_pad(width):
    return -(-(width - 1) // SUBLANES) * SUBLANES


def _dwconv(tail_ref, keep_tail, v_ref, w_ref, w_row, width, out_ref, *, chained):
    tm, d = v_ref.shape
    pad = tail_ref.shape[0]
    first = pad - (width - 1)
    g = CHAIN_TILES
    nq = (first + width - 1) // SUBLANES + 1
    assert pad == SUBLANES * (nq - 1) and (tm // SUBLANES) % g == 0
    sub = jax.lax.broadcasted_iota(jnp.int32, (g, SUBLANES, LANES), 1)
    taps_of = [[(q, SUBLANES * q + r - first) for q in range(nq)
                if 0 <= SUBLANES * q + r - first < width] for r in range(SUBLANES)]
    zero = None
    for c0 in range(0, d, LANES):
        lanes = slice(c0, c0 + LANES)
        w = [jnp.broadcast_to(w_ref[w_row + k:w_row + k + 1, lanes], (SUBLANES, LANES))
             for k in range(width)]
        win = tail_ref[:, lanes]
        if keep_tail is not None:
            win = jnp.where(keep_tail, win, 0.0)
        win = win.reshape(nq - 1, SUBLANES, LANES)
        halo = None
        for j0 in range(0, tm // SUBLANES, g):
            new = v_ref[j0 * SUBLANES:(j0 + g) * SUBLANES, lanes].reshape(g, SUBLANES, LANES)
            if chained:
                new = jnp.concatenate([new[:g - 1], _after(new[g - 1], zero)[None]], axis=0)
            x = jnp.concatenate([win, new], axis=0)
            acc, new_halo = None, []
            for r in range(SUBLANES):
                if not taps_of[r]:
                    continue
                lo = 0 if (r == 0 or halo is None) else 1
                n = g if r == 0 else g + 1
                y = None
                for q, k in taps_of[r]:
                    term = x[q + lo:q + n] * w[k][None]
                    y = term if y is None else y + term
                if r == 0:
                    z = y
                else:
                    rot = pltpu.roll(y, SUBLANES - r, axis=1)
                    if halo is not None:
                        rot = jnp.concatenate([halo[len(new_halo)], rot], axis=0)
                    new_halo.append(rot[g:g + 1])
                    z = jnp.where(sub < SUBLANES - r, rot[0:g], rot[1:g + 1])
                acc = z if acc is None else acc + z
            out_ref[j0 * SUBLANES:(j0 + g) * SUBLANES, lanes] = acc.reshape(g * SUBLANES, LANES)
            halo = new_halo
            win = x[g:g + nq - 1]
            if chained:
                zero = _zero_after(acc, CHAIN_XLU_HOPS)
    return zero


def _ffn_glu_kernel(x_ref, vec_ref, wg_ref, wu_ref, wd_ref, w1_ref, xs_ref, *rest,
                    g_row, v0, width, tiles_per_seq, n_tiles):
    n_cast = (len(rest) - 4) // 2
    srcs, dsts = rest[:n_cast], rest[n_cast + 4:]
    x1_ref, v_ref, nb_ref, ys_ref = rest[n_cast:n_cast + 4]
    s = pl.program_id(0)
    tm, d = x_ref.shape
    vec = lambda i: vec_ref[v0 + i:v0 + i + 1, :]
    ffn = functools.partial(_ffn_out, vec_ref=vec_ref, wg_ref=wg_ref, wu_ref=wu_ref, wd_ref=wd_ref,
                            g_row=g_row, final_row=None)

    @pl.when(s < n_tiles)
    def _():
        x1 = ffn(x_ref[...])
        x1_ref[...] = x1
        hn = _rms(x1, vec(0)).astype(jnp.bfloat16)
        a = _dot(hn, w1_ref[:, 0:d]) + vec(1)
        g = _dot(hn, w1_ref[:, d:2 * d]) + vec(2)
        v_ref[...] = a * jax.nn.sigmoid(g)
        _side_cast(srcs, dsts)

        @pl.when(s % tiles_per_seq == tiles_per_seq - 1)
        def _():
            nb_ref[0, 0] = v_ref[tm - (width - 1):tm, :]

    @pl.when(s == n_tiles)
    def _():
        ys_ref[...] = ffn(xs_ref[...])


def _ffn_glu(x, xs, vecs, wg, wu, wd, w1, *, g_row, v0, width, seq, cast=()):
    m, d = x.shape
    tm = min(TOKEN_TILE, seq)
    n_tiles, tiles_per_seq = m // tm, seq // tm
    tile = lambda s: jnp.minimum(s, n_tiles - 1)
    row = pl.BlockSpec((tm, d), lambda s: (tile(s), 0))
    c_in, c_out, c_shape = _side_cast_specs(cast, n_tiles, tile)
    res = pl.pallas_call(
        functools.partial(_ffn_glu_kernel, g_row=g_row, v0=v0, width=width,
                          tiles_per_seq=tiles_per_seq, n_tiles=n_tiles),
        grid=(n_tiles + 1,),
        in_specs=[row, _resident(vecs.shape), _resident(wg.shape), _resident(wu.shape),
                  _resident(wd.shape), _resident(w1.shape), _resident(xs.shape)] + c_in,
        out_specs=[row, row,
                   pl.BlockSpec((1, 1, width - 1, d),
                                lambda s: (0, tile(s) // tiles_per_seq, 0, 0)),
                   _resident(xs.shape)] + c_out,
        out_shape=[jax.ShapeDtypeStruct((m, d), jnp.float32),
                   jax.ShapeDtypeStruct((m, d), jnp.float32),
                   jax.ShapeDtypeStruct((1, m // seq, width - 1, d), jnp.float32),
                   jax.ShapeDtypeStruct(xs.shape, jnp.float32)] + c_shape,
        compiler_params=_params(1),
        name="ffn1_glu",
    )(x, vecs, wg, wu, wd, w1, xs, *[w for w, _ in cast])
    return res[0], res[1], res[2], res[3], res[4:]


def _conv_ln_swish(tail_ref, keep_tail, v_ref, vec_ref, v0, width, conv_ref, c_ref, *, chained):
    tm, d = v_ref.shape
    pad = tail_ref.shape[0]
    vec = lambda i: vec_ref[v0 + i:v0 + i + 1, :]
    zero = _dwconv(tail_ref, keep_tail, v_ref, vec_ref, v0 + 7, width, conv_ref, chained=chained)
    for r0 in range(0, tm, LN_ROWS):
        cin = conv_ref[r0:r0 + LN_ROWS, :]
        if chained:
            corner = _after(cin[0:SUBLANES, 0:LANES], zero)
            top = jnp.concatenate([corner, cin[0:SUBLANES, LANES:]], axis=1)
            cin = jnp.concatenate([top, cin[SUBLANES:]], axis=0)
        c = _layer_norm(cin + vec(3), vec(4), vec(5))
        c = c * jax.nn.sigmoid(c)
        c_ref[r0:r0 + LN_ROWS, :] = c.astype(jnp.bfloat16)
        if chained:
            zero = _zero_after(c.reshape(LN_ROWS // SUBLANES, SUBLANES, d)[:, :, 0:LANES], 0)
    tail_ref[...] = v_ref[tm - pad:tm, :]


def _conv_ffn_kernel(x1_ref, v_ref, vec_ref, w2_ref, wg_ref, wu_ref, wd_ref, xs_ref, *rest,
                     g_row, final_row, v0, width, tiles_per_seq):
    n_cast = (len(rest) - 5) // 2
    srcs, (o_ref, ys_ref), dsts = rest[:n_cast], rest[n_cast:n_cast + 2], rest[n_cast + 2:-3]
    c_buf, tail_ref, conv_ref = rest[-3:]
    s = pl.program_id(0)
    slot = s % 2
    stage = functools.partial(_conv_ln_swish, tail_ref, v_ref=v_ref, vec_ref=vec_ref, v0=v0,
                              width=width, conv_ref=conv_ref)
    ffn = functools.partial(_ffn_out, vec_ref=vec_ref, wg_ref=wg_ref, wu_ref=wu_ref, wd_ref=wd_ref,
                            g_row=g_row, final_row=final_row)

    @pl.when(s == 0)
    def _():
        tail_ref[...] = jnp.zeros_like(tail_ref)
        stage(keep_tail=None, c_ref=c_buf.at[0], chained=False)
        ys_ref[...] = ffn(xs_ref[...])

    @pl.when(s > 0)
    def _():
        c = c_buf[1 - slot]
        x2 = x1_ref[...] + _dot(c, w2_ref[...]) + vec_ref[v0 + 6:v0 + 7, :]
        o_ref[...] = ffn(x2)
        _side_cast(srcs, dsts)
        stage(keep_tail=s % tiles_per_seq != 0, c_ref=c_buf.at[slot], chained=True)


def _conv_ffn(x1, v, xs, vecs, w2, wg, wu, wd, *, g_row, final_row, v0, width, seq, cast=()):
    m, d = x1.shape
    tm = min(TOKEN_TILE, seq)
    n_tiles = m // tm
    pad = _conv_pad(width)
    tile = lambda s: jnp.maximum(s - 1, 0)
    behind = pl.BlockSpec((tm, d), lambda s: (tile(s), 0))
    ahead = pl.BlockSpec((tm, d), lambda s: (jnp.minimum(s, n_tiles - 1), 0))
    c_in, c_out, c_shape = _side_cast_specs(cast, n_tiles, tile)
    res = pl.pallas_call(
        functools.partial(_conv_ffn_kernel, g_row=g_row, final_row=final_row, v0=v0, width=width,
                          tiles_per_seq=seq // tm),
        grid=(n_tiles + 1,),
        in_specs=[behind, ahead, _resident(vecs.shape), _resident(w2.shape), _resident(wg.shape),
                  _resident(wu.shape), _resident(wd.shape), _resident(xs.shape)] + c_in,
        out_specs=[behind, _resident(xs.shape)] + c_out,
        out_shape=[jax.ShapeDtypeStruct((m, d), jnp.float32),
                   jax.ShapeDtypeStruct(xs.shape, jnp.float32)] + c_shape,
        scratch_shapes=[pltpu.VMEM((2, tm, d), jnp.bfloat16), pltpu.VMEM((pad, d), jnp.float32),
                        pltpu.VMEM((tm, d), jnp.float32)],
        compiler_params=_params(1),
        name="conv_ffn2",
    )(x1, v, vecs, w2, wg, wu, wd, xs, *[w for w, _ in cast])
    return res[0], res[1], res[2:]


def _mixer_b_prompt_kernel(x_ref, vec_ref, win_ref, wout_ref, o_ref, nb_ref, u_ref, tail_ref,
                           conv_ref, *, width, v0):
    t = pl.program_id(1)
    _, tm, d = x_ref.shape
    pad = tail_ref.shape[0]

    @pl.when(t == 0)
    def _():
        tail_ref[...] = jnp.zeros_like(tail_ref)

    x = x_ref[0]
    hn = _rms(x, vec_ref[v0:v0 + 1, :]).astype(jnp.bfloat16)
    b_gate = _dot(hn, win_ref[:, 0:d])
    c_gate = _dot(hn, win_ref[:, d:2 * d])
    h = _dot(hn, win_ref[:, 2 * d:3 * d])
    u_ref[...] = c_gate * h

    _dwconv(tail_ref, None, u_ref, vec_ref, v0 + 1, width, conv_ref, chained=False)

    y = (b_gate * conv_ref[...]).astype(jnp.bfloat16)
    o_ref[0] = x + _dot(y, wout_ref[...])
    tail_ref[...] = u_ref[tm - pad:tm, :]

    @pl.when(t == pl.num_programs(1) - 1)
    def _():
        nb_ref[0, 0] = u_ref[tm - (width - 1):tm, :]


def _mixer_b_prompt(x, vecs, win, wout, *, width, v0):
    bsz, seq, d = x.shape
    tm = min(TOKEN_TILE, seq)
    pad = _conv_pad(width)
    row = pl.BlockSpec((1, tm, d), lambda b, t: (b, t, 0))
    return pl.pallas_call(
        functools.partial(_mixer_b_prompt_kernel, width=width, v0=v0),
        grid=(bsz, seq // tm),
        in_specs=[row, _resident(vecs.shape), _resident(win.shape), _resident(wout.shape)],
        out_specs=[row, pl.BlockSpec((1, 1, width - 1, d), lambda b, t: (0, b, 0, 0))],
        out_shape=[jax.ShapeDtypeStruct((bsz, seq, d), jnp.float32),
                   jax.ShapeDtypeStruct((1, bsz, width - 1, d), jnp.float32)],
        scratch_shapes=[pltpu.VMEM((tm, d), jnp.float32), pltpu.VMEM((pad, d), jnp.float32),
                        pltpu.VMEM((tm, d), jnp.float32)],
        compiler_params=_params(2),
        name="mixer_b_prompt",
    )(x, vecs, win, wout)


def _state_specs(n, hist, d):
    st_in = pl.BlockSpec((None, n, d), lambda k: (jnp.minimum(k, hist - 1), 0, 0))
    st_out = pl.BlockSpec((None, n, d), lambda k: (jnp.maximum(k - 1, 0), 0, 0))
    return st_in, st_out


def _mixer_a_sample_kernel(x_ref, st_ref, vec_ref, w1_ref, w2_ref, o_ref, ns_ref, acc_ref, *, v0):
    k = pl.program_id(0)
    last = pl.num_programs(0) - 1
    d = x_ref.shape[-1]
    vec = lambda i: vec_ref[v0 + i:v0 + i + 1, :]
    tap = vec_ref[pl.ds(v0 + 7 + k, 1), :]

    @pl.when(k == 0)
    def _():
        acc_ref[...] = jnp.zeros_like(acc_ref)

    @pl.when(k < last)
    def _():
        row = st_ref[...]
        acc_ref[...] += row * tap
        ns_ref[...] = row

    @pl.when(k == last)
    def _():
        x = x_ref[...]
        hn = _rms(x, vec(0)).astype(jnp.bfloat16)
        a = _dot(hn, w1_ref[:, 0:d]) + vec(1)
        g = _dot(hn, w1_ref[:, d:2 * d]) + vec(2)
        v = a * jax.nn.sigmoid(g)
        ns_ref[...] = v
        c = _layer_norm(acc_ref[...] + v * tap + vec(3), vec(4), vec(5))
        c = (c * jax.nn.sigmoid(c)).astype(jnp.bfloat16)
        o_ref[...] = x + _dot(c, w2_ref[...]) + vec(6)


def _mixer_a_sample(x, state, vecs, w1, w2, *, width, v0):
    n, d = x.shape
    st_in, st_out = _state_specs(n, width - 1, d)
    return pl.pallas_call(
        functools.partial(_mixer_a_sample_kernel, v0=v0),
        grid=(width,),
        in_specs=[_resident((n, d)), st_in, _resident(vecs.shape), _resident(w1.shape),
                  _resident(w2.shape)],
        out_specs=[pl.BlockSpec((n, d), lambda k: (0, 0)), st_out],
        out_shape=[jax.ShapeDtypeStruct((n, d), jnp.float32),
                   jax.ShapeDtypeStruct((width - 1, n, d), jnp.float32)],
        scratch_shapes=[pltpu.VMEM((n, d), jnp.float32)],
        compiler_params=_params(1),
        name="mixer_a_sample",
    )(x, state, vecs, w1, w2)


def _mixer_b_sample_kernel(x_ref, st_ref, vec_ref, win_ref, wout_ref, o_ref, ns_ref, acc_ref, *, v0):
    k = pl.program_id(0)
    last = pl.num_programs(0) - 1
    d = x_ref.shape[-1]
    tap = vec_ref[pl.ds(v0 + 1 + k, 1), :]

    @pl.when(k == 0)
    def _():
        acc_ref[...] = jnp.zeros_like(acc_ref)

    @pl.when(k < last)
    def _():
        row = st_ref[...]
        acc_ref[...] += row * tap
        ns_ref[...] = row

    @pl.when(k == last)
    def _():
        x = x_ref[...]
        hn = _rms(x, vec_ref[v0:v0 + 1, :]).astype(jnp.bfloat16)
        b_gate = _dot(hn, win_ref[:, 0:d])
        c_gate = _dot(hn, win_ref[:, d:2 * d])
        h = _dot(hn, win_ref[:, 2 * d:3 * d])
        u = c_gate * h
        ns_ref[...] = u
        y = (b_gate * (acc_ref[...] + u * tap)).astype(jnp.bfloat16)
        o_ref[...] = x + _dot(y, wout_ref[...])


def _mixer_b_sample(x, state, vecs, win, wout, *, width, v0):
    n, d = x.shape
    st_in, st_out = _state_specs(n, width - 1, d)
    return pl.pallas_call(
        functools.partial(_mixer_b_sample_kernel, v0=v0),
        grid=(width,),
        in_specs=[_resident((n, d)), st_in, _resident(vecs.shape), _resident(win.shape),
                  _resident(wout.shape)],
        out_specs=[pl.BlockSpec((n, d), lambda k: (0, 0)), st_out],
        out_shape=[jax.ShapeDtypeStruct((n, d), jnp.float32),
                   jax.ShapeDtypeStruct((width - 1, n, d), jnp.float32)],
        scratch_shapes=[pltpu.VMEM((n, d), jnp.float32)],
        compiler_params=_params(1),
        name="mixer_b_sample",
    )(x, state, vecs, win, wout)


def _pack_vectors(d, groups):
    rows, starts, n = [], [], 0
    for v in groups:
        v = v.reshape(-1, d)
        starts.append(n)
        rows.append(v)
        n += v.shape[0]
    fill = -n % SUBLANES
    if fill:
        rows.append(jnp.zeros((fill, d), jnp.float32))
    return jnp.concatenate(rows, axis=0), starts


def _cat(per_layer):
    return per_layer[0] if len(per_layer) == 1 else jnp.concatenate(per_layer, axis=0)


def kernel(x_prompt, x_sample, state_conv_a, state_conv_b, ffn1_norm, ffn1_w_gate, ffn1_w_up, ffn1_w_down, mix_norm, ffn2_norm, ffn2_w_gate, ffn2_w_up, ffn2_w_down, final_norm, a_w_pw1, a_b_pw1, a_w_dw, a_b_dw, a_ln_g, a_ln_b, a_w_pw2, a_b_pw2, b_w_in, b_w_conv, b_w_out):
    depth = ffn1_norm.shape[0]
    bsz, seq, d = x_prompt.shape
    n_dec = x_sample.shape[0]
    width_a = a_w_dw.shape[1]
    width_b = b_w_conv.shape[1]

    groups = [ffn1_norm, ffn2_norm, final_norm]
    for i in range(depth):
        j = i // 2
        if i % 2 == 0:
            groups += [mix_norm[i], a_b_pw1[j], a_b_dw[j], a_ln_g[j], a_ln_b[j], a_b_pw2[j], a_w_dw[j]]
        else:
            groups += [mix_norm[i], b_w_conv[j]]
    vecs, starts = _pack_vectors(d, groups)
    ffn1_row, ffn2_row, final_row = starts[0], starts[1], starts[2]
    mix_rows, p = [], 3
    for i in range(depth):
        mix_rows.append(starts[p])
        p += 7 if i % 2 == 0 else 2

    def ffn1_src(i):
        return [(ffn1_w_gate, i), (ffn1_w_up, i), (ffn1_w_down, i)]

    def ffn2_src(i):
        return [(ffn2_w_gate, i), (ffn2_w_up, i), (ffn2_w_down, i)]

    stage_src = []
    for i in range(depth):
        j = i // 2
        if i % 2 == 0:
            stage_src += [ffn1_src(i) + [(a_w_pw1, j)], [(a_w_pw2, j)] + ffn2_src(i)]
        else:
            stage_src += [ffn1_src(i), [(b_w_in, j), (b_w_out, j)] + ffn2_src(i)]
    stage_src.append([])
    ready = list(_cast_weights(stage_src[0]))

    xp = x_prompt.reshape(bsz * seq, d)
    xs = x_sample.reshape(n_dec, d)
    new_a_p, new_a_s, new_b_p, new_b_s = [], [], [], []
    for i in range(depth):
        j = i // 2
        fin = final_row if i == depth - 1 else None
        nxt1, nxt2 = stage_src[2 * i + 1], stage_src[2 * i + 2]
        if i % 2 == 0:
            kw = dict(v0=mix_rows[i], width=width_a)
            wg, wu, wd, w1 = ready
            x1, v, nb, xs, ready = _ffn_glu(xp, xs, vecs, wg, wu, wd, w1, g_row=ffn1_row + i,
                                            seq=seq, cast=nxt1, **kw)
            w2, wg, wu, wd = ready
            state = jnp.transpose(state_conv_a[j], (1, 0, 2))
            xs, ns = _mixer_a_sample(xs, state, vecs, w1, w2, **kw)
            xp, xs, ready = _conv_ffn(x1, v, xs, vecs, w2, wg, wu, wd, g_row=ffn2_row + i,
                                      final_row=fin, seq=seq, cast=nxt2, **kw)
            new_a_p.append(nb)
            new_a_s.append(jnp.transpose(ns, (1, 0, 2))[None])
        else:
            kw = dict(v0=mix_rows[i], width=width_b)
            wg, wu, wd = ready
            xp, xs, ready = _ffn(xp, xs, vecs, wg, wu, wd, g_row=ffn1_row + i, cast=nxt1,
                                 name=f"ffn1_{i}")
            win, wout, wg, wu, wd = ready
            xp3, nb = _mixer_b_prompt(xp.reshape(bsz, seq, d), vecs, win, wout, **kw)
            state = jnp.transpose(state_conv_b[j], (1, 0, 2))
            xs, ns = _mixer_b_sample(xs, state, vecs, win, wout, **kw)
            xp, xs, ready = _ffn(xp3.reshape(bsz * seq, d), xs, vecs, wg, wu, wd,
                                 g_row=ffn2_row + i, final_row=fin, cast=nxt2, name=f"ffn2_{i}")
            new_b_p.append(nb)
            new_b_s.append(jnp.transpose(ns, (1, 0, 2))[None])

    return (xp.reshape(bsz, seq, d), xs.reshape(n_dec, 1, d),
            _cat(new_a_p), _cat(new_a_s), _cat(new_b_p), _cat(new_b_s))
```

```python
import functools

import jax
import jax.numpy as jnp
from jax.experimental import pallas as pl
from jax.experimental.pallas import tpu as pltpu

RMS_EPS = 1e-6
LN_EPS = 1e-5
FFN_RES_WEIGHT = 0.5

SUBLANES = 8
LANES = 128
BF16_ROWS = 16
VMEM_LIMIT_BYTES = 56 * 1024 * 1024

TOKEN_TILE = 512
FFN_TOKEN_TILE = 1024
FF_CHUNK = 512
STATE_ROWS = 6
CAST_STEPS = 16


def _rms(x, g):
    return x * jax.lax.rsqrt(jnp.mean(x * x, axis=-1, keepdims=True) + RMS_EPS) * g


def _layer_norm(x, g, b):
    mu = jnp.mean(x, axis=-1, keepdims=True)
    xc = x - mu
    var = jnp.mean(xc * xc, axis=-1, keepdims=True)
    return xc * jax.lax.rsqrt(var + LN_EPS) * g + b


def _dot(a, b):
    return jnp.dot(a, b, preferred_element_type=jnp.float32)


def _resident(shape):
    zeros = (0,) * len(shape)
    return pl.BlockSpec(shape, lambda *_: zeros, pipeline_mode=pl.Buffered(1))


def _params(n_grid):
    return pltpu.CompilerParams(
        dimension_semantics=("arbitrary",) * n_grid,
        vmem_limit_bytes=VMEM_LIMIT_BYTES,
    )


def _cast_kernel(*refs):
    n = len(refs) // 2
    for src, dst in zip(refs[:n], refs[n:]):
        dst[...] = src[...].astype(jnp.bfloat16)


def _cast_weights(items):
    in_specs, out_specs, out_shape = [], [], []
    for w, layer in items:
        _, rows, cols = w.shape
        assert rows % (CAST_STEPS * BF16_ROWS) == 0
        blk = rows // CAST_STEPS
        in_specs.append(pl.BlockSpec((None, blk, cols), lambda i, layer=layer: (layer, i, 0)))
        out_specs.append(pl.BlockSpec((blk, cols), lambda i: (i, 0)))
        out_shape.append(jax.ShapeDtypeStruct((rows, cols), jnp.bfloat16))
    return pl.pallas_call(
        _cast_kernel,
        grid=(CAST_STEPS,),
        in_specs=in_specs,
        out_specs=out_specs,
        out_shape=out_shape,
        compiler_params=_params(1),
        name="cast_weights",
    )(*[w for w, _ in items])


def _side_cast_specs(items, n_tiles, tile_of_step):
    in_specs, out_specs, out_shape = [], [], []
    for w, layer in items:
        _, rows, cols = w.shape
        n_blocks = n_tiles
        while rows % (n_blocks * BF16_ROWS) or n_tiles % n_blocks:
            n_blocks -= 1
        blk, per = rows // n_blocks, n_tiles // n_blocks
        in_specs.append(pl.BlockSpec(
            (None, blk, cols), lambda s, layer=layer, per=per: (layer, tile_of_step(s) // per, 0)))
        out_specs.append(pl.BlockSpec((blk, cols), lambda s, per=per: (tile_of_step(s) // per, 0)))
        out_shape.append(jax.ShapeDtypeStruct((rows, cols), jnp.bfloat16))
    return in_specs, out_specs, out_shape


def _side_cast(srcs, dsts):
    for src, dst in zip(srcs, dsts):
        dst[...] = src[...].astype(jnp.bfloat16)


def _ffn_body(x, g, wg_ref, wu_ref, wd_ref):
    d_ff = wg_ref.shape[1]
    h = _rms(x, g).astype(jnp.bfloat16)
    y = None
    for c0 in range(0, d_ff, FF_CHUNK):
        c1 = min(c0 + FF_CHUNK, d_ff)
        gate = _dot(h, wg_ref[:, c0:c1])
        up = _dot(h, wu_ref[:, c0:c1])
        a = (gate * jax.nn.sigmoid(gate) * up).astype(jnp.bfloat16)
        part = _dot(a, wd_ref[c0:c1, :])
        y = part if y is None else y + part
    return x + FFN_RES_WEIGHT * y


def _ffn_out(x, vec_ref, wg_ref, wu_ref, wd_ref, g_row, final_row):
    out = _ffn_body(x, vec_ref[g_row:g_row + 1, :], wg_ref, wu_ref, wd_ref)
    if final_row is not None:
        out = _rms(out, vec_ref[final_row:final_row + 1, :])
    return out


def _ffn_kernel(x_ref, vec_ref, wg_ref, wu_ref, wd_ref, xs_ref, *rest, g_row, final_row, n_tiles):
    n_cast = (len(rest) - 2) // 2
    srcs, (o_ref, ys_ref), dsts = rest[:n_cast], rest[n_cast:n_cast + 2], rest[n_cast + 2:]
    s = pl.program_id(0)
    ffn = functools.partial(_ffn_out, vec_ref=vec_ref, wg_ref=wg_ref, wu_ref=wu_ref, wd_ref=wd_ref,
                            g_row=g_row, final_row=final_row)

    @pl.when(s < n_tiles)
    def _():
        o_ref[...] = ffn(x_ref[...])
        _side_cast(srcs, dsts)

    @pl.when(s == n_tiles)
    def _():
        ys_ref[...] = ffn(xs_ref[...])


def _ffn(x, xs, vecs, wg, wu, wd, *, g_row, final_row=None, cast=(), name):
    m, d = x.shape
    tm = min(FFN_TOKEN_TILE, m)
    n_tiles = m // tm
    tile = lambda s: jnp.minimum(s, n_tiles - 1)
    row = pl.BlockSpec((tm, d), lambda s: (tile(s), 0))
    c_in, c_out, c_shape = _side_cast_specs(cast, n_tiles, tile)
    res = pl.pallas_call(
        functools.partial(_ffn_kernel, g_row=g_row, final_row=final_row, n_tiles=n_tiles),
        grid=(n_tiles + 1,),
        in_specs=[row, _resident(vecs.shape), _resident(wg.shape), _resident(wu.shape),
                  _resident(wd.shape), _resident(xs.shape)] + c_in,
        out_specs=[row, _resident(xs.shape)] + c_out,
        out_shape=[jax.ShapeDtypeStruct((m, d), jnp.float32),
                   jax.ShapeDtypeStruct(xs.shape, jnp.float32)] + c_shape,
        compiler_params=_params(1),
        name=name,
    )(x, vecs, wg, wu, wd, xs, *[w for w, _ in cast])
    return res[0], res[1], res[2:]


CHAIN_TILES = 8
CHAIN_XLU_HOPS = 1
LN_ROWS = 64


def _zero_after(tiles, xlu_hops):
    b = jax.lax.bitcast_convert_type(tiles, jnp.uint32)
    t = b[0]
    for i in range(1, b.shape[0]):
        t = t | b[i]
    t = (t >> 16) >> 16
    for _ in range(xlu_hops):
        t = pltpu.roll(t, 1, axis=1)
    return t


def _after(x, zero):
    if zero is None:
        return x
    return jnp.where(zero == 0, x, 0.0)


def _conv_pad(width):
    return -(-(width - 1) // SUBLANES) * SUBLANES


def _dwconv(tail_ref, keep_tail, v_ref, w_ref, w_row, width, out_ref, *, chained):
    tm, d = v_ref.shape
    pad = tail_ref.shape[0]
    first = pad - (width - 1)
    g = CHAIN_TILES
    nq = (first + width - 1) // SUBLANES + 1
    assert pad == SUBLANES * (nq - 1) and (tm // SUBLANES) % g == 0
    sub = jax.lax.broadcasted_iota(jnp.int32, (g, SUBLANES, LANES), 1)
    taps_of = [[(q, SUBLANES * q + r - first) for q in range(nq)
                if 0 <= SUBLANES * q + r - first < width] for r in range(SUBLANES)]
    zero = None
    for c0 in range(0, d, LANES):
        lanes = slice(c0, c0 + LANES)
        win = tail_ref[:, lanes]
        if keep_tail is not None:
            win = jnp.where(keep_tail, win, 0.0)
        win = win.reshape(nq - 1, SUBLANES, LANES)
        halo = None
        for j0 in range(0, tm // SUBLANES, g):
            new = v_ref[j0 * SUBLANES:(j0 + g) * SUBLANES, lanes].reshape(g, SUBLANES, LANES)
            if chained:
                new = jnp.concatenate([new[:g - 1], _after(new[g - 1], zero)[None]], axis=0)
            x = jnp.concatenate([win, new], axis=0)
            acc, new_halo = None, []
            for r in range(SUBLANES):
                if not taps_of[r]:
                    continue
                lo = 0 if (r == 0 or halo is None) else 1
                n = g if r == 0 else g + 1
                y = None
                for q, k in taps_of[r]:
                    term = x[q + lo:q + n] * w_ref[w_row + k:w_row + k + 1, lanes][None]
                    y = term if y is None else y + term
                if r == 0:
                    z = y
                else:
                    rot = pltpu.roll(y, SUBLANES - r, axis=1)
                    if halo is not None:
                        rot = jnp.concatenate([halo[len(new_halo)], rot], axis=0)
                    new_halo.append(rot[g:g + 1])
                    z = jnp.where(sub < SUBLANES - r, rot[0:g], rot[1:g + 1])
                acc = z if acc is None else acc + z
            out_ref[j0 * SUBLANES:(j0 + g) * SUBLANES, lanes] = acc.reshape(g * SUBLANES, LANES)
            halo = new_halo
            win = x[g:g + nq - 1]
            if chained:
                zero = _zero_after(acc, CHAIN_XLU_HOPS)
    return zero


def _ffn_glu_kernel(x_ref, vec_ref, wg_ref, wu_ref, wd_ref, w1_ref, xs_ref, *rest,
                    g_row, v0, width, tiles_per_seq, n_tiles):
    n_cast = (len(rest) - 4) // 2
    srcs, dsts = rest[:n_cast], rest[n_cast + 4:]
    x1_ref, v_ref, nb_ref, ys_ref = rest[n_cast:n_cast + 4]
    s = pl.program_id(0)
    tm, d = x_ref.shape
    vec = lambda i: vec_ref[v0 + i:v0 + i + 1, :]
    ffn = functools.partial(_ffn_out, vec_ref=vec_ref, wg_ref=wg_ref, wu_ref=wu_ref, wd_ref=wd_ref,
                            g_row=g_row, final_row=None)

    @pl.when(s < n_tiles)
    def _():
        x1 = ffn(x_ref[...])
        x1_ref[...] = x1
        hn = _rms(x1, vec(0)).astype(jnp.bfloat16)
        a = _dot(hn, w1_ref[:, 0:d]) + vec(1)
        g = _dot(hn, w1_ref[:, d:2 * d]) + vec(2)
        v_ref[...] = a * jax.nn.sigmoid(g)
        _side_cast(srcs, dsts)

        @pl.when(s % tiles_per_seq == tiles_per_seq - 1)
        def _():
            nb_ref[0, 0] = v_ref[tm - (width - 1):tm, :]

    @pl.when(s == n_tiles)
    def _():
        ys_ref[...] = ffn(xs_ref[...])


def _ffn_glu(x, xs, vecs, wg, wu, wd, w1, *, g_row, v0, width, seq, cast=()):
    m, d = x.shape
    tm = min(TOKEN_TILE, seq)
    n_tiles, tiles_per_seq = m // tm, seq // tm
    tile = lambda s: jnp.minimum(s, n_tiles - 1)
    row = pl.BlockSpec((tm, d), lambda s: (tile(s), 0))
    c_in, c_out, c_shape = _side_cast_specs(cast, n_tiles, tile)
    res = pl.pallas_call(
        functools.partial(_ffn_glu_kernel, g_row=g_row, v0=v0, width=width,
                          tiles_per_seq=tiles_per_seq, n_tiles=n_tiles),
        grid=(n_tiles + 1,),
        in_specs=[row, _resident(vecs.shape), _resident(wg.shape), _resident(wu.shape),
                  _resident(wd.shape), _resident(w1.shape), _resident(xs.shape)] + c_in,
        out_specs=[row, row,
                   pl.BlockSpec((1, 1, width - 1, d),
                                lambda s: (0, tile(s) // tiles_per_seq, 0, 0)),
                   _resident(xs.shape)] + c_out,
        out_shape=[jax.ShapeDtypeStruct((m, d), jnp.float32),
                   jax.ShapeDtypeStruct((m, d), jnp.float32),
                   jax.ShapeDtypeStruct((1, m // seq, width - 1, d), jnp.float32),
                   jax.ShapeDtypeStruct(xs.shape, jnp.float32)] + c_shape,
        compiler_params=_params(1),
        name="ffn1_glu",
    )(x, vecs, wg, wu, wd, w1, xs, *[w for w, _ in cast])
    return res[0], res[1], res[2], res[3], res[4:]


def _conv_ln_swish(tail_ref, keep_tail, v_ref, vec_ref, v0, width, conv_ref, c_ref, *, chained):
    tm, d = v_ref.shape
    pad = tail_ref.shape[0]
    vec = lambda i: vec_ref[v0 + i:v0 + i + 1, :]
    zero = _dwconv(tail_ref, keep_tail, v_ref, vec_ref, v0 + 7, width, conv_ref, chained=chained)
    for r0 in range(0, tm, LN_ROWS):
        cin = conv_ref[r0:r0 + LN_ROWS, :]
        if chained:
            corner = _after(cin[0:SUBLANES, 0:LANES], zero)
            top = jnp.concatenate([corner, cin[0:SUBLANES, LANES:]], axis=1)
            cin = jnp.concatenate([top, cin[SUBLANES:]], axis=0)
        c = _layer_norm(cin + vec(3), vec(4), vec(5))
        c = c * jax.nn.sigmoid(c)
        c_ref[r0:r0 + LN_ROWS, :] = c.astype(jnp.bfloat16)
        if chained:
            zero = _zero_after(c.reshape(LN_ROWS // SUBLANES, SUBLANES, d)[:, :, 0:LANES], 0)
    tail_ref[...] = v_ref[tm - pad:tm, :]


def _conv_ffn_kernel(x1_ref, v_ref, vec_ref, w2_ref, wg_ref, wu_ref, wd_ref, xs_ref, *rest,
                     g_row, final_row, v0, width, tiles_per_seq):
    n_cast = (len(rest) - 5) // 2
    srcs, (o_ref, ys_ref), dsts = rest[:n_cast], rest[n_cast:n_cast + 2], rest[n_cast + 2:-3]
    c_buf, tail_ref, conv_ref = rest[-3:]
    s = pl.program_id(0)
    slot = s % 2
    stage = functools.partial(_conv_ln_swish, tail_ref, v_ref=v_ref, vec_ref=vec_ref, v0=v0,
                              width=width, conv_ref=conv_ref)
    ffn = functools.partial(_ffn_out, vec_ref=vec_ref, wg_ref=wg_ref, wu_ref=wu_ref, wd_ref=wd_ref,
                            g_row=g_row, final_row=final_row)

    @pl.when(s == 0)
    def _():
        tail_ref[...] = jnp.zeros_like(tail_ref)
        ys_ref[...] = ffn(xs_ref[...])
        stage(keep_tail=None, c_ref=c_buf.at[0], chained=True)

    @pl.when(s > 0)
    def _():
        c = c_buf[1 - slot]
        x2 = x1_ref[...] + _dot(c, w2_ref[...]) + vec_ref[v0 + 6:v0 + 7, :]
        o_ref[...] = ffn(x2)
        _side_cast(srcs, dsts)
        stage(keep_tail=s % tiles_per_seq != 0, c_ref=c_buf.at[slot], chained=True)


def _conv_ffn(x1, v, xs, vecs, w2, wg, wu, wd, *, g_row, final_row, v0, width, seq, cast=()):
    m, d = x1.shape
    tm = min(TOKEN_TILE, seq)
    n_tiles = m // tm
    pad = _conv_pad(width)
    tile = lambda s: jnp.maximum(s - 1, 0)
    behind = pl.BlockSpec((tm, d), lambda s: (tile(s), 0))
    ahead = pl.BlockSpec((tm, d), lambda s: (jnp.minimum(s, n_tiles - 1), 0))
    c_in, c_out, c_shape = _side_cast_specs(cast, n_tiles, tile)
    res = pl.pallas_call(
        functools.partial(_conv_ffn_kernel, g_row=g_row, final_row=final_row, v0=v0, width=width,
                          tiles_per_seq=seq // tm),
        grid=(n_tiles + 1,),
        in_specs=[behind, ahead, _resident(vecs.shape), _resident(w2.shape), _resident(wg.shape),
                  _resident(wu.shape), _resident(wd.shape), _resident(xs.shape)] + c_in,
        out_specs=[behind, _resident(xs.shape)] + c_out,
        out_shape=[jax.ShapeDtypeStruct((m, d), jnp.float32),
                   jax.ShapeDtypeStruct(xs.shape, jnp.float32)] + c_shape,
        scratch_shapes=[pltpu.VMEM((2, tm, d), jnp.bfloat16), pltpu.VMEM((pad, d), jnp.float32),
                        pltpu.VMEM((tm, d), jnp.float32)],
        compiler_params=_params(1),
        name="conv_ffn2",
    )(x1, v, vecs, w2, wg, wu, wd, xs, *[w for w, _ in cast])
    return res[0], res[1], res[2:]


def _mixer_b_prompt_kernel(x_ref, vec_ref, win_ref, wout_ref, o_ref, nb_ref, u_ref, tail_ref,
                           conv_ref, *, width, v0):
    t = pl.program_id(1)
    _, tm, d = x_ref.shape
    pad = tail_ref.shape[0]

    @pl.when(t == 0)
    def _():
        tail_ref[...] = jnp.zeros_like(tail_ref)

    x = x_ref[0]
    hn = _rms(x, vec_ref[v0:v0 + 1, :]).astype(jnp.bfloat16)
    b_gate = _dot(hn, win_ref[:, 0:d])
    c_gate = _dot(hn, win_ref[:, d:2 * d])
    h = _dot(hn, win_ref[:, 2 * d:3 * d])
    u_ref[...] = c_gate * h

    _dwconv(tail_ref, None, u_ref, vec_ref, v0 + 1, width, conv_ref, chained=False)

    y = (b_gate * conv_ref[...]).astype(jnp.bfloat16)
    o_ref[0] = x + _dot(y, wout_ref[...])
    tail_ref[...] = u_ref[tm - pad:tm, :]

    @pl.when(t == pl.num_programs(1) - 1)
    def _():
        nb_ref[0, 0] = u_ref[tm - (width - 1):tm, :]


def _mixer_b_prompt(x, vecs, win, wout, *, width, v0):
    bsz, seq, d = x.shape
    tm = min(TOKEN_TILE, seq)
    pad = _conv_pad(width)
    row = pl.BlockSpec((1, tm, d), lambda b, t: (b, t, 0))
    return pl.pallas_call(
        functools.partial(_mixer_b_prompt_kernel, width=width, v0=v0),
        grid=(bsz, seq // tm),
        in_specs=[row, _resident(vecs.shape), _resident(win.shape), _resident(wout.shape)],
        out_specs=[row, pl.BlockSpec((1, 1, width - 1, d), lambda b, t: (0, b, 0, 0))],
        out_shape=[jax.ShapeDtypeStruct((bsz, seq, d), jnp.float32),
                   jax.ShapeDtypeStruct((1, bsz, width - 1, d), jnp.float32)],
        scratch_shapes=[pltpu.VMEM((tm, d), jnp.float32), pltpu.VMEM((pad, d), jnp.float32),
                        pltpu.VMEM((tm, d), jnp.float32)],
        compiler_params=_params(2),
        name="mixer_b_prompt",
    )(x, vecs, win, wout)


def _state_specs(n, hist, d):
    st_in = pl.BlockSpec((None, n, d), lambda k: (jnp.minimum(k, hist - 1), 0, 0))
    st_out = pl.BlockSpec((None, n, d), lambda k: (jnp.maximum(k - 1, 0), 0, 0))
    return st_in, st_out


def _mixer_a_sample_kernel(x_ref, st_ref, nxt_ref, vec_ref, w1_ref, w2_ref, o_ref, ns_ref, acc_ref,
                           *, v0, width):
    k = pl.program_id(0)
    last = pl.num_programs(0) - 1
    d = x_ref.shape[-1]
    rows = st_ref.shape[0]
    vec = lambda i: vec_ref[v0 + i:v0 + i + 1, :]

    @pl.when(k == 0)
    def _():
        acc_ref[...] = jnp.zeros_like(acc_ref)

    part = None
    for i in range(rows):
        term = st_ref[i] * vec_ref[pl.ds(v0 + 7 + rows * k + i, 1), :]
        part = term if part is None else part + term
    acc_ref[...] += part
    ns_ref[0:rows - 1] = st_ref[1:rows]

    @pl.when(k < last)
    def _():
        ns_ref[rows - 1] = nxt_ref[...]

    @pl.when(k == last)
    def _():
        x = x_ref[...]
        hn = _rms(x, vec(0)).astype(jnp.bfloat16)
        a = _dot(hn, w1_ref[:, 0:d]) + vec(1)
        g = _dot(hn, w1_ref[:, d:2 * d]) + vec(2)
        v = a * jax.nn.sigmoid(g)
        ns_ref[rows - 1] = v
        c = _layer_norm(acc_ref[...] + v * vec(7 + width - 1) + vec(3), vec(4), vec(5))
        c = (c * jax.nn.sigmoid(c)).astype(jnp.bfloat16)
        o_ref[...] = x + _dot(c, w2_ref[...]) + vec(6)


def _mixer_a_sample(x, state, vecs, w1, w2, *, width, v0):
    n, d = x.shape
    hist = width - 1
    assert hist % STATE_ROWS == 0
    blocks = pl.BlockSpec((STATE_ROWS, n, d), lambda k: (k, 0, 0))
    nxt = pl.BlockSpec((None, n, d), lambda k: (jnp.minimum(STATE_ROWS * (k + 1), hist - 1), 0, 0))
    return pl.pallas_call(
        functools.partial(_mixer_a_sample_kernel, v0=v0, width=width),
        grid=(hist // STATE_ROWS,),
        in_specs=[_resident((n, d)), blocks, nxt, _resident(vecs.shape), _resident(w1.shape),
                  _resident(w2.shape)],
        out_specs=[pl.BlockSpec((n, d), lambda k: (0, 0)), blocks],
        out_shape=[jax.ShapeDtypeStruct((n, d), jnp.float32),
                   jax.ShapeDtypeStruct((hist, n, d), jnp.float32)],
        scratch_shapes=[pltpu.VMEM((n, d), jnp.float32)],
        compiler_params=_params(1),
        name="mixer_a_sample",
    )(x, state, state, vecs, w1, w2)


def _mixer_b_sample_kernel(x_ref, st_ref, vec_ref, win_ref, wout_ref, o_ref, ns_ref, acc_ref, *, v0):
    k = pl.program_id(0)
    last = pl.num_programs(0) - 1
    d = x_ref.shape[-1]
    tap = vec_ref[pl.ds(v0 + 1 + k, 1), :]

    @pl.when(k == 0)
    def _():
        acc_ref[...] = jnp.zeros_like(acc_ref)

    @pl.when(k < last)
    def _():
        row = st_ref[...]
        acc_ref[...] += row * tap
        ns_ref[...] = row

    @pl.when(k == last)
    def _():
        x = x_ref[...]
        hn = _rms(x, vec_ref[v0:v0 + 1, :]).astype(jnp.bfloat16)
        b_gate = _dot(hn, win_ref[:, 0:d])
        c_gate = _dot(hn, win_ref[:, d:2 * d])
        h = _dot(hn, win_ref[:, 2 * d:3 * d])
        u = c_gate * h
        ns_ref[...] = u
        y = (b_gate * (acc_ref[...] + u * tap)).astype(jnp.bfloat16)
        o_ref[...] = x + _dot(y, wout_ref[...])


def _mixer_b_sample(x, state, vecs, win, wout, *, width, v0):
    n, d = x.shape
    st_in, st_out = _state_specs(n, width - 1, d)
    return pl.pallas_call(
        functools.partial(_mixer_b_sample_kernel, v0=v0),
        grid=(width,),
        in_specs=[_resident((n, d)), st_in, _resident(vecs.shape), _resident(win.shape),
                  _resident(wout.shape)],
        out_specs=[pl.BlockSpec((n, d), lambda k: (0, 0)), st_out],
        out_shape=[jax.ShapeDtypeStruct((n, d), jnp.float32),
                   jax.ShapeDtypeStruct((width - 1, n, d), jnp.float32)],
        scratch_shapes=[pltpu.VMEM((n, d), jnp.float32)],
        compiler_params=_params(1),
        name="mixer_b_sample",
    )(x, state, vecs, win, wout)


def _pack_vectors(d, groups):
    rows, starts, n = [], [], 0
    for v in groups:
        v = v.reshape(-1, d)
        starts.append(n)
        rows.append(v)
        n += v.shape[0]
    fill = -n % SUBLANES
    if fill:
        rows.append(jnp.zeros((fill, d), jnp.float32))
    return jnp.concatenate(rows, axis=0), starts


def _cat(per_layer):
    return per_layer[0] if len(per_layer) == 1 else jnp.concatenate(per_layer, axis=0)


def kernel(x_prompt, x_sample, state_conv_a, state_conv_b, ffn1_norm, ffn1_w_gate, ffn1_w_up, ffn1_w_down, mix_norm, ffn2_norm, ffn2_w_gate, ffn2_w_up, ffn2_w_down, final_norm, a_w_pw1, a_b_pw1, a_w_dw, a_b_dw, a_ln_g, a_ln_b, a_w_pw2, a_b_pw2, b_w_in, b_w_conv, b_w_out):
    depth = ffn1_norm.shape[0]
    bsz, seq, d = x_prompt.shape
    n_dec = x_sample.shape[0]
    width_a = a_w_dw.shape[1]
    width_b = b_w_conv.shape[1]

    groups = [ffn1_norm, ffn2_norm, final_norm]
    for i in range(depth):
        j = i // 2
        if i % 2 == 0:
            groups += [mix_norm[i], a_b_pw1[j], a_b_dw[j], a_ln_g[j], a_ln_b[j], a_b_pw2[j], a_w_dw[j]]
        else:
            groups += [mix_norm[i], b_w_conv[j]]
    vecs, starts = _pack_vectors(d, groups)
    ffn1_row, ffn2_row, final_row = starts[0], starts[1], starts[2]
    mix_rows, p = [], 3
    for i in range(depth):
        mix_rows.append(starts[p])
        p += 7 if i % 2 == 0 else 2

    def ffn1_src(i):
        return [(ffn1_w_gate, i), (ffn1_w_up, i), (ffn1_w_down, i)]

    def ffn2_src(i):
        return [(ffn2_w_gate, i), (ffn2_w_up, i), (ffn2_w_down, i)]

    stage_src = []
    for i in range(depth):
        j = i // 2
        if i % 2 == 0:
            stage_src += [ffn1_src(i) + [(a_w_pw1, j)], [(a_w_pw2, j)] + ffn2_src(i)]
        else:
            stage_src += [ffn1_src(i), [(b_w_in, j), (b_w_out, j)] + ffn2_src(i)]
    stage_src.append([])
    ready = list(_cast_weights(stage_src[0]))

    xp = x_prompt.reshape(bsz * seq, d)
    xs = x_sample.reshape(n_dec, d)
    new_a_p, new_a_s, new_b_p, new_b_s = [], [], [], []
    for i in range(depth):
        j = i // 2
        fin = final_row if i == depth - 1 else None
        nxt1, nxt2 = stage_src[2 * i + 1], stage_src[2 * i + 2]
        if i % 2 == 0:
            kw = dict(v0=mix_rows[i], width=width_a)
            wg, wu, wd, w1 = ready
            x1, v, nb, xs, ready = _ffn_glu(xp, xs, vecs, wg, wu, wd, w1, g_row=ffn1_row + i,
                                            seq=seq, cast=nxt1, **kw)
            w2, wg, wu, wd = ready
            state = jnp.transpose(state_conv_a[j], (1, 0, 2))
            xs, ns = _mixer_a_sample(xs, state, vecs, w1, w2, **kw)
            xp, xs, ready = _conv_ffn(x1, v, xs, vecs, w2, wg, wu, wd, g_row=ffn2_row + i,
                                      final_row=fin, seq=seq, cast=nxt2, **kw)
            new_a_p.append(nb)
            new_a_s.append(jnp.transpose(ns, (1, 0, 2))[None])
        else:
            kw = dict(v0=mix_rows[i], width=width_b)
            wg, wu, wd = ready
            xp, xs, ready = _ffn(xp, xs, vecs, wg, wu, wd, g_row=ffn1_row + i, cast=nxt1,
                                 name=f"ffn1_{i}")
            win, wout, wg, wu, wd = ready
            xp3, nb = _mixer_b_prompt(xp.reshape(bsz, seq, d), vecs, win, wout, **kw)
            state = jnp.transpose(state_conv_b[j], (1, 0, 2))
            xs, ns = _mixer_b_sample(xs, state, vecs, win, wout, **kw)
            xp, xs, ready = _ffn(xp3.reshape(bsz * seq, d), xs, vecs, wg, wu, wd,
                                 g_row=ffn2_row + i, final_row=fin, cast=nxt2, name=f"ffn2_{i}")
            new_b_p.append(nb)
            new_b_s.append(jnp.transpose(ns, (1, 0, 2))[None])

    return (xp.reshape(bsz, seq, d), xs.reshape(n_dec, 1, d),
            _cat(new_a_p), _cat(new_a_s), _cat(new_b_p), _cat(new_b_s))
```

```python
import functools

import jax
import jax.numpy as jnp
from jax.experimental import pallas as pl
from jax.experimental.pallas import tpu as pltpu

RMS_EPS = 1e-6
LN_EPS = 1e-5
FFN_RES_WEIGHT = 0.5

SUBLANES = 8
LANES = 128
BF16_ROWS = 16
VMEM_LIMIT_BYTES = 56 * 1024 * 1024

TOKEN_TILE = 512
FF_CHUNK = 512
STATE_ROWS = 6
CAST_STEPS = 16


def _rms(x, g):
    return x * jax.lax.rsqrt(jnp.mean(x * x, axis=-1, keepdims=True) + RMS_EPS) * g


def _layer_norm(x, g, b):
    mu = jnp.mean(x, axis=-1, keepdims=True)
    xc = x - mu
    var = jnp.mean(xc * xc, axis=-1, keepdims=True)
    return xc * jax.lax.rsqrt(var + LN_EPS) * g + b


def _dot(a, b):
    return jnp.dot(a, b, preferred_element_type=jnp.float32)


def _resident(shape):
    zeros = (0,) * len(shape)
    return pl.BlockSpec(shape, lambda *_: zeros, pipeline_mode=pl.Buffered(1))


def _params(n_grid):
    return pltpu.CompilerParams(
        dimension_semantics=("arbitrary",) * n_grid,
        vmem_limit_bytes=VMEM_LIMIT_BYTES,
    )


def _cast_kernel(*refs):
    n = len(refs) // 2
    for src, dst in zip(refs[:n], refs[n:]):
        dst[...] = src[...].astype(jnp.bfloat16)


def _cast_weights(items):
    in_specs, out_specs, out_shape = [], [], []
    for w, layer in items:
        _, rows, cols = w.shape
        assert rows % (CAST_STEPS * BF16_ROWS) == 0
        blk = rows // CAST_STEPS
        in_specs.append(pl.BlockSpec((None, blk, cols), lambda i, layer=layer: (layer, i, 0)))
        out_specs.append(pl.BlockSpec((blk, cols), lambda i: (i, 0)))
        out_shape.append(jax.ShapeDtypeStruct((rows, cols), jnp.bfloat16))
    return pl.pallas_call(
        _cast_kernel,
        grid=(CAST_STEPS,),
        in_specs=in_specs,
        out_specs=out_specs,
        out_shape=out_shape,
        compiler_params=_params(1),
        name="cast_weights",
    )(*[w for w, _ in items])


def _side_cast_specs(items, n_tiles, tile_of_step):
    in_specs, out_specs, out_shape = [], [], []
    for w, layer in items:
        _, rows, cols = w.shape
        n_blocks = n_tiles
        while rows % (n_blocks * BF16_ROWS) or n_tiles % n_blocks:
            n_blocks -= 1
        blk, per = rows // n_blocks, n_tiles // n_blocks
        in_specs.append(pl.BlockSpec(
            (None, blk, cols), lambda s, layer=layer, per=per: (layer, tile_of_step(s) // per, 0)))
        out_specs.append(pl.BlockSpec((blk, cols), lambda s, per=per: (tile_of_step(s) // per, 0)))
        out_shape.append(jax.ShapeDtypeStruct((rows, cols), jnp.bfloat16))
    return in_specs, out_specs, out_shape


def _side_cast(srcs, dsts):
    for src, dst in zip(srcs, dsts):
        dst[...] = src[...].astype(jnp.bfloat16)


def _ffn_body(x, g, wg_ref, wu_ref, wd_ref):
    d_ff = wg_ref.shape[1]
    h = _rms(x, g).astype(jnp.bfloat16)
    y = None
    for c0 in range(0, d_ff, FF_CHUNK):
        c1 = min(c0 + FF_CHUNK, d_ff)
        gate = _dot(h, wg_ref[:, c0:c1])
        up = _dot(h, wu_ref[:, c0:c1])
        a = (gate * jax.nn.sigmoid(gate) * up).astype(jnp.bfloat16)
        part = _dot(a, wd_ref[c0:c1, :])
        y = part if y is None else y + part
    return x + FFN_RES_WEIGHT * y


def _ffn_out(x, vec_ref, wg_ref, wu_ref, wd_ref, g_row, final_row):
    out = _ffn_body(x, vec_ref[g_row:g_row + 1, :], wg_ref, wu_ref, wd_ref)
    if final_row is not None:
        out = _rms(out, vec_ref[final_row:final_row + 1, :])
    return out


def _ffn_kernel(x_ref, vec_ref, wg_ref, wu_ref, wd_ref, xs_ref, *rest, g_row, final_row, n_tiles):
    n_cast = (len(rest) - 2) // 2
    srcs, (o_ref, ys_ref), dsts = rest[:n_cast], rest[n_cast:n_cast + 2], rest[n_cast + 2:]
    s = pl.program_id(0)
    ffn = functools.partial(_ffn_out, vec_ref=vec_ref, wg_ref=wg_ref, wu_ref=wu_ref, wd_ref=wd_ref,
                            g_row=g_row, final_row=final_row)

    @pl.when(s < n_tiles)
    def _():
        o_ref[...] = ffn(x_ref[...])
        _side_cast(srcs, dsts)

    @pl.when(s == n_tiles)
    def _():
        ys_ref[...] = ffn(xs_ref[...])


def _ffn(x, xs, vecs, wg, wu, wd, *, g_row, final_row=None, cast=(), name):
    m, d = x.shape
    tm = min(TOKEN_TILE, m)
    n_tiles = m // tm
    tile = lambda s: jnp.minimum(s, n_tiles - 1)
    row = pl.BlockSpec((tm, d), lambda s: (tile(s), 0))
    c_in, c_out, c_shape = _side_cast_specs(cast, n_tiles, tile)
    res = pl.pallas_call(
        functools.partial(_ffn_kernel, g_row=g_row, final_row=final_row, n_tiles=n_tiles),
        grid=(n_tiles + 1,),
        in_specs=[row, _resident(vecs.shape), _resident(wg.shape), _resident(wu.shape),
                  _resident(wd.shape), _resident(xs.shape)] + c_in,
        out_specs=[row, _resident(xs.shape)] + c_out,
        out_shape=[jax.ShapeDtypeStruct((m, d), jnp.float32),
                   jax.ShapeDtypeStruct(xs.shape, jnp.float32)] + c_shape,
        compiler_params=_params(1),
        name=name,
    )(x, vecs, wg, wu, wd, xs, *[w for w, _ in cast])
    return res[0], res[1], res[2:]


CHAIN_TILES = 8
CHAIN_XLU_HOPS = 1
LN_ROWS = 64


def _zero_after(tiles, xlu_hops):
    b = jax.lax.bitcast_convert_type(tiles, jnp.uint32)
    t = b[0]
    for i in range(1, b.shape[0]):
        t = t | b[i]
    t = (t >> 16) >> 16
    for _ in range(xlu_hops):
        t = pltpu.roll(t, 1, axis=1)
    return t


def _after(x, zero):
    if zero is None:
        return x
    return jnp.where(zero == 0, x, 0.0)


def _conv_pad(width):
    return -(-(width - 1) // SUBLANES) * SUBLANES


def _dwconv(tail_ref, keep_tail, v_ref, w_ref, w_row, width, out_ref, *, chained):
    tm, d = v_ref.shape
    pad = tail_ref.shape[0]
    first = pad - (width - 1)
    g = CHAIN_TILES
    nq = (first + width - 1) // SUBLANES + 1
    assert pad == SUBLANES * (nq - 1) and (tm // SUBLANES) % g == 0
    sub = jax.lax.broadcasted_iota(jnp.int32, (g, SUBLANES, LANES), 1)
    taps_of = [[(q, SUBLANES * q + r - first) for q in range(nq)
                if 0 <= SUBLANES * q + r - first < width] for r in range(SUBLANES)]
    zero = None
    for c0 in range(0, d, LANES):
        lanes = slice(c0, c0 + LANES)
        w = [jnp.broadcast_to(w_ref[w_row + k:w_row + k + 1, lanes], (SUBLANES, LANES))
             for k in range(width)]
        win = tail_ref[:, lanes]
        if keep_tail is not None:
            win = jnp.where(keep_tail, win, 0.0)
        win = win.reshape(nq - 1, SUBLANES, LANES)
        halo = None
        for j0 in range(0, tm // SUBLANES, g):
            new = v_ref[j0 * SUBLANES:(j0 + g) * SUBLANES, lanes].reshape(g, SUBLANES, LANES)
            if chained:
                new = jnp.concatenate([new[:g - 1], _after(new[g - 1], zero)[None]], axis=0)
            x = jnp.concatenate([win, new], axis=0)
            acc, new_halo = None, []
            for r in range(SUBLANES):
                if not taps_of[r]:
                    continue
                lo = 0 if (r == 0 or halo is None) else 1
                n = g if r == 0 else g + 1
                y = None
                for q, k in taps_of[r]:
                    term = x[q + lo:q + n] * w[k][None]
                    y = term if y is None else y + term
                if r == 0:
                    z = y
                else:
                    rot = pltpu.roll(y, SUBLANES - r, axis=1)
                    if halo is not None:
                        rot = jnp.concatenate([halo[len(new_halo)], rot], axis=0)
                    new_halo.append(rot[g:g + 1])
                    z = jnp.where(sub < SUBLANES - r, rot[0:g], rot[1:g + 1])
                acc = z if acc is None else acc + z
            out_ref[j0 * SUBLANES:(j0 + g) * SUBLANES, lanes] = acc.reshape(g * SUBLANES, LANES)
            halo = new_halo
            win = x[g:g + nq - 1]
            if chained:
                zero = _zero_after(acc, CHAIN_XLU_HOPS)
    return zero


def _ffn_glu_kernel(x_ref, vec_ref, wg_ref, wu_ref, wd_ref, w1_ref, xs_ref, *rest,
                    g_row, v0, width, tiles_per_seq, n_tiles):
    n_cast = (len(rest) - 4) // 2
    srcs, dsts = rest[:n_cast], rest[n_cast + 4:]
    x1_ref, v_ref, nb_ref, ys_ref = rest[n_cast:n_cast + 4]
    s = pl.program_id(0)
    tm, d = x_ref.shape
    vec = lambda i: vec_ref[v0 + i:v0 + i + 1, :]
    ffn = functools.partial(_ffn_out, vec_ref=vec_ref, wg_ref=wg_ref, wu_ref=wu_ref, wd_ref=wd_ref,
                            g_row=g_row, final_row=None)

    @pl.when(s < n_tiles)
    def _():
        x1 = ffn(x_ref[...])
        x1_ref[...] = x1
        hn = _rms(x1, vec(0)).astype(jnp.bfloat16)
        a = _dot(hn, w1_ref[:, 0:d]) + vec(1)
        g = _dot(hn, w1_ref[:, d:2 * d]) + vec(2)
        v_ref[...] = a * jax.nn.sigmoid(g)
        _side_cast(srcs, dsts)

        @pl.when(s % tiles_per_seq == tiles_per_seq - 1)
        def _():
            nb_ref[0, 0] = v_ref[tm - (width - 1):tm, :]

    @pl.when(s == n_tiles)
    def _():
        ys_ref[...] = ffn(xs_ref[...])


def _ffn_glu(x, xs, vecs, wg, wu, wd, w1, *, g_row, v0, width, seq, cast=()):
    m, d = x.shape
    tm = min(TOKEN_TILE, seq)
    n_tiles, tiles_per_seq = m // tm, seq // tm
    tile = lambda s: jnp.minimum(s, n_tiles - 1)
    row = pl.BlockSpec((tm, d), lambda s: (tile(s), 0))
    c_in, c_out, c_shape = _side_cast_specs(cast, n_tiles, tile)
    res = pl.pallas_call(
        functools.partial(_ffn_glu_kernel, g_row=g_row, v0=v0, width=width,
                          tiles_per_seq=tiles_per_seq, n_tiles=n_tiles),
        grid=(n_tiles + 1,),
        in_specs=[row, _resident(vecs.shape), _resident(wg.shape), _resident(wu.shape),
                  _resident(wd.shape), _resident(w1.shape), _resident(xs.shape)] + c_in,
        out_specs=[row, row,
                   pl.BlockSpec((1, 1, width - 1, d),
                                lambda s: (0, tile(s) // tiles_per_seq, 0, 0)),
                   _resident(xs.shape)] + c_out,
        out_shape=[jax.ShapeDtypeStruct((m, d), jnp.float32),
                   jax.ShapeDtypeStruct((m, d), jnp.float32),
                   jax.ShapeDtypeStruct((1, m // seq, width - 1, d), jnp.float32),
                   jax.ShapeDtypeStruct(xs.shape, jnp.float32)] + c_shape,
        compiler_params=_params(1),
        name="ffn1_glu",
    )(x, vecs, wg, wu, wd, w1, xs, *[w for w, _ in cast])
    return res[0], res[1], res[2], res[3], res[4:]


def _conv_ln_swish(tail_ref, keep_tail, v_ref, vec_ref, v0, width, conv_ref, c_ref, *, chained):
    tm, d = v_ref.shape
    pad = tail_ref.shape[0]
    vec = lambda i: vec_ref[v0 + i:v0 + i + 1, :]
    zero = _dwconv(tail_ref, keep_tail, v_ref, vec_ref, v0 + 7, width, conv_ref, chained=chained)
    for r0 in range(0, tm, LN_ROWS):
        cin = conv_ref[r0:r0 + LN_ROWS, :]
        if chained:
            corner = _after(cin[0:SUBLANES, 0:LANES], zero)
            top = jnp.concatenate([corner, cin[0:SUBLANES, LANES:]], axis=1)
            cin = jnp.concatenate([top, cin[SUBLANES:]], axis=0)
        c = _layer_norm(cin + vec(3), vec(4), vec(5))
        c = c * jax.nn.sigmoid(c)
        c_ref[r0:r0 + LN_ROWS, :] = c.astype(jnp.bfloat16)
        if chained:
            zero = _zero_after(c.reshape(LN_ROWS // SUBLANES, SUBLANES, d)[:, :, 0:LANES], 0)
    tail_ref[...] = v_ref[tm - pad:tm, :]


def _conv_ffn_kernel(x1_ref, v_ref, vec_ref, w2_ref, wg_ref, wu_ref, wd_ref, xs_ref, *rest,
                     g_row, final_row, v0, width, tiles_per_seq):
    n_cast = (len(rest) - 5) // 2
    srcs, (o_ref, ys_ref), dsts = rest[:n_cast], rest[n_cast:n_cast + 2], rest[n_cast + 2:-3]
    c_buf, tail_ref, conv_ref = rest[-3:]
    s = pl.program_id(0)
    slot = s % 2
    stage = functools.partial(_conv_ln_swish, tail_ref, v_ref=v_ref, vec_ref=vec_ref, v0=v0,
                              width=width, conv_ref=conv_ref)
    ffn = functools.partial(_ffn_out, vec_ref=vec_ref, wg_ref=wg_ref, wu_ref=wu_ref, wd_ref=wd_ref,
                            g_row=g_row, final_row=final_row)

    @pl.when(s == 0)
    def _():
        tail_ref[...] = jnp.zeros_like(tail_ref)
        stage(keep_tail=None, c_ref=c_buf.at[0], chained=False)
        ys_ref[...] = ffn(xs_ref[...])

    @pl.when(s > 0)
    def _():
        c = c_buf[1 - slot]
        x2 = x1_ref[...] + _dot(c, w2_ref[...]) + vec_ref[v0 + 6:v0 + 7, :]
        o_ref[...] = ffn(x2)
        _side_cast(srcs, dsts)
        stage(keep_tail=s % tiles_per_seq != 0, c_ref=c_buf.at[slot], chained=True)


def _conv_ffn(x1, v, xs, vecs, w2, wg, wu, wd, *, g_row, final_row, v0, width, seq, cast=()):
    m, d = x1.shape
    tm = min(TOKEN_TILE, seq)
    n_tiles = m // tm
    pad = _conv_pad(width)
    tile = lambda s: jnp.maximum(s - 1, 0)
    behind = pl.BlockSpec((tm, d), lambda s: (tile(s), 0))
    ahead = pl.BlockSpec((tm, d), lambda s: (jnp.minimum(s, n_tiles - 1), 0))
    c_in, c_out, c_shape = _side_cast_specs(cast, n_tiles, tile)
    res = pl.pallas_call(
        functools.partial(_conv_ffn_kernel, g_row=g_row, final_row=final_row, v0=v0, width=width,
                          tiles_per_seq=seq // tm),
        grid=(n_tiles + 1,),
        in_specs=[behind, ahead, _resident(vecs.shape), _resident(w2.shape), _resident(wg.shape),
                  _resident(wu.shape), _resident(wd.shape), _resident(xs.shape)] + c_in,
        out_specs=[behind, _resident(xs.shape)] + c_out,
        out_shape=[jax.ShapeDtypeStruct((m, d), jnp.float32),
                   jax.ShapeDtypeStruct(xs.shape, jnp.float32)] + c_shape,
        scratch_shapes=[pltpu.VMEM((2, tm, d), jnp.bfloat16), pltpu.VMEM((pad, d), jnp.float32),
                        pltpu.VMEM((tm, d), jnp.float32)],
        compiler_params=_params(1),
        name="conv_ffn2",
    )(x1, v, vecs, w2, wg, wu, wd, xs, *[w for w, _ in cast])
    return res[0], res[1], res[2:]


def _mixer_b_prompt_kernel(x_ref, vec_ref, win_ref, wout_ref, o_ref, nb_ref, u_ref, tail_ref,
                           conv_ref, *, width, v0):
    t = pl.program_id(1)
    _, tm, d = x_ref.shape
    pad = tail_ref.shape[0]

    @pl.when(t == 0)
    def _():
        tail_ref[...] = jnp.zeros_like(tail_ref)

    x = x_ref[0]
    hn = _rms(x, vec_ref[v0:v0 + 1, :]).astype(jnp.bfloat16)
    b_gate = _dot(hn, win_ref[:, 0:d])
    c_gate = _dot(hn, win_ref[:, d:2 * d])
    h = _dot(hn, win_ref[:, 2 * d:3 * d])
    u_ref[...] = c_gate * h

    _dwconv(tail_ref, None, u_ref, vec_ref, v0 + 1, width, conv_ref, chained=False)

    y = (b_gate * conv_ref[...]).astype(jnp.bfloat16)
    o_ref[0] = x + _dot(y, wout_ref[...])
    tail_ref[...] = u_ref[tm - pad:tm, :]

    @pl.when(t == pl.num_programs(1) - 1)
    def _():
        nb_ref[0, 0] = u_ref[tm - (width - 1):tm, :]


def _mixer_b_prompt(x, vecs, win, wout, *, width, v0):
    bsz, seq, d = x.shape
    tm = min(TOKEN_TILE, seq)
    pad = _conv_pad(width)
    row = pl.BlockSpec((1, tm, d), lambda b, t: (b, t, 0))
    return pl.pallas_call(
        functools.partial(_mixer_b_prompt_kernel, width=width, v0=v0),
        grid=(bsz, seq // tm),
        in_specs=[row, _resident(vecs.shape), _resident(win.shape), _resident(wout.shape)],
        out_specs=[row, pl.BlockSpec((1, 1, width - 1, d), lambda b, t: (0, b, 0, 0))],
        out_shape=[jax.ShapeDtypeStruct((bsz, seq, d), jnp.float32),
                   jax.ShapeDtypeStruct((1, bsz, width - 1, d), jnp.float32)],
        scratch_shapes=[pltpu.VMEM((tm, d), jnp.float32), pltpu.VMEM((pad, d), jnp.float32),
                        pltpu.VMEM((tm, d), jnp.float32)],
        compiler_params=_params(2),
        name="mixer_b_prompt",
    )(x, vecs, win, wout)


def _mixer_a_sample_kernel(x_ref, st_ref, nxt_ref, vec_ref, w1_ref, w2_ref, o_ref, ns_ref, acc_ref,
                           *, v0, width):
    k = pl.program_id(0)
    last = pl.num_programs(0) - 1
    d = x_ref.shape[-1]
    rows = st_ref.shape[0]
    vec = lambda i: vec_ref[v0 + i:v0 + i + 1, :]

    @pl.when(k == 0)
    def _():
        acc_ref[...] = jnp.zeros_like(acc_ref)

    part = None
    for i in range(rows):
        term = st_ref[i] * vec_ref[pl.ds(v0 + 7 + rows * k + i, 1), :]
        part = term if part is None else part + term
    acc_ref[...] += part
    ns_ref[0:rows - 1] = st_ref[1:rows]

    @pl.when(k < last)
    def _():
        ns_ref[rows - 1] = nxt_ref[...]

    @pl.when(k == last)
    def _():
        x = x_ref[...]
        hn = _rms(x, vec(0)).astype(jnp.bfloat16)
        a = _dot(hn, w1_ref[:, 0:d]) + vec(1)
        g = _dot(hn, w1_ref[:, d:2 * d]) + vec(2)
        v = a * jax.nn.sigmoid(g)
        ns_ref[rows - 1] = v
        c = _layer_norm(acc_ref[...] + v * vec(7 + width - 1) + vec(3), vec(4), vec(5))
        c = (c * jax.nn.sigmoid(c)).astype(jnp.bfloat16)
        o_ref[...] = x + _dot(c, w2_ref[...]) + vec(6)


def _mixer_a_sample(x, state, vecs, w1, w2, *, width, v0):
    n, d = x.shape
    hist = width - 1
    assert hist % STATE_ROWS == 0
    blocks = pl.BlockSpec((STATE_ROWS, n, d), lambda k: (k, 0, 0))
    nxt = pl.BlockSpec((None, n, d), lambda k: (jnp.minimum(STATE_ROWS * (k + 1), hist - 1), 0, 0))
    return pl.pallas_call(
        functools.partial(_mixer_a_sample_kernel, v0=v0, width=width),
        grid=(hist // STATE_ROWS,),
        in_specs=[_resident((n, d)), blocks, nxt, _resident(vecs.shape), _resident(w1.shape),
                  _resident(w2.shape)],
        out_specs=[pl.BlockSpec((n, d), lambda k: (0, 0)), blocks],
        out_shape=[jax.ShapeDtypeStruct((n, d), jnp.float32),
                   jax.ShapeDtypeStruct((hist, n, d), jnp.float32)],
        scratch_shapes=[pltpu.VMEM((n, d), jnp.float32)],
        compiler_params=_params(1),
        name="mixer_a_sample",
    )(x, state, state, vecs, w1, w2)


def _mixer_b_sample_kernel(x_ref, st_ref, vec_ref, win_ref, wout_ref, o_ref, ns_ref, *, v0):
    d = x_ref.shape[-1]
    hist = st_ref.shape[0]
    tap = lambda k: vec_ref[v0 + 1 + k:v0 + 2 + k, :]
    x = x_ref[...]
    hn = _rms(x, vec_ref[v0:v0 + 1, :]).astype(jnp.bfloat16)
    b_gate = _dot(hn, win_ref[:, 0:d])
    c_gate = _dot(hn, win_ref[:, d:2 * d])
    h = _dot(hn, win_ref[:, 2 * d:3 * d])
    u = c_gate * h
    conv = st_ref[0] * tap(0)
    for k in range(1, hist):
        conv = conv + st_ref[k] * tap(k)
    conv = conv + u * tap(hist)
    ns_ref[0:hist - 1] = st_ref[1:hist]
    ns_ref[hist - 1] = u
    y = (b_gate * conv).astype(jnp.bfloat16)
    o_ref[...] = x + _dot(y, wout_ref[...])


def _mixer_b_sample(x, state, vecs, win, wout, *, width, v0):
    n, d = x.shape
    return pl.pallas_call(
        functools.partial(_mixer_b_sample_kernel, v0=v0),
        out_shape=[jax.ShapeDtypeStruct((n, d), jnp.float32),
                   jax.ShapeDtypeStruct((width - 1, n, d), jnp.float32)],
        compiler_params=pltpu.CompilerParams(vmem_limit_bytes=VMEM_LIMIT_BYTES),
        name="mixer_b_sample",
    )(x, state, vecs, win, wout)


def _pack_vectors(d, groups):
    rows, starts, n = [], [], 0
    for v in groups:
        v = v.reshape(-1, d)
        starts.append(n)
        rows.append(v)
        n += v.shape[0]
    fill = -n % SUBLANES
    if fill:
        rows.append(jnp.zeros((fill, d), jnp.float32))
    return jnp.concatenate(rows, axis=0), starts


def _cat(per_layer):
    return per_layer[0] if len(per_layer) == 1 else jnp.concatenate(per_layer, axis=0)


def kernel(x_prompt, x_sample, state_conv_a, state_conv_b, ffn1_norm, ffn1_w_gate, ffn1_w_up, ffn1_w_down, mix_norm, ffn2_norm, ffn2_w_gate, ffn2_w_up, ffn2_w_down, final_norm, a_w_pw1, a_b_pw1, a_w_dw, a_b_dw, a_ln_g, a_ln_b, a_w_pw2, a_b_pw2, b_w_in, b_w_conv, b_w_out):
    depth = ffn1_norm.shape[0]
    bsz, seq, d = x_prompt.shape
    n_dec = x_sample.shape[0]
    width_a = a_w_dw.shape[1]
    width_b = b_w_conv.shape[1]

    groups = [ffn1_norm, ffn2_norm, final_norm]
    for i in range(depth):
        j = i // 2
        if i % 2 == 0:
            groups += [mix_norm[i], a_b_pw1[j], a_b_dw[j], a_ln_g[j], a_ln_b[j], a_b_pw2[j], a_w_dw[j]]
        else:
            groups += [mix_norm[i], b_w_conv[j]]
    vecs, starts = _pack_vectors(d, groups)
    ffn1_row, ffn2_row, final_row = starts[0], starts[1], starts[2]
    mix_rows, p = [], 3
    for i in range(depth):
        mix_rows.append(starts[p])
        p += 7 if i % 2 == 0 else 2

    def ffn1_src(i):
        return [(ffn1_w_gate, i), (ffn1_w_up, i), (ffn1_w_down, i)]

    def ffn2_src(i):
        return [(ffn2_w_gate, i), (ffn2_w_up, i), (ffn2_w_down, i)]

    stage_src = []
    for i in range(depth):
        j = i // 2
        if i % 2 == 0:
            stage_src += [ffn1_src(i) + [(a_w_pw1, j)], [(a_w_pw2, j)] + ffn2_src(i)]
        else:
            stage_src += [ffn1_src(i), [(b_w_in, j), (b_w_out, j)] + ffn2_src(i)]
    stage_src.append([])
    ready = list(_cast_weights(stage_src[0]))

    xp = x_prompt.reshape(bsz * seq, d)
    xs = x_sample.reshape(n_dec, d)
    new_a_p, new_a_s, new_b_p, new_b_s = [], [], [], []
    for i in range(depth):
        j = i // 2
        fin = final_row if i == depth - 1 else None
        nxt1, nxt2 = stage_src[2 * i + 1], stage_src[2 * i + 2]
        if i % 2 == 0:
            kw = dict(v0=mix_rows[i], width=width_a)
            wg, wu, wd, w1 = ready
            x1, v, nb, xs, ready = _ffn_glu(xp, xs, vecs, wg, wu, wd, w1, g_row=ffn1_row + i,
                                            seq=seq, cast=nxt1, **kw)
            w2, wg, wu, wd = ready
            state = jnp.transpose(state_conv_a[j], (1, 0, 2))
            xs, ns = _mixer_a_sample(xs, state, vecs, w1, w2, **kw)
            xp, xs, ready = _conv_ffn(x1, v, xs, vecs, w2, wg, wu, wd, g_row=ffn2_row + i,
                                      final_row=fin, seq=seq, cast=nxt2, **kw)
            new_a_p.append(nb)
            new_a_s.append(jnp.transpose(ns, (1, 0, 2))[None])
        else:
            kw = dict(v0=mix_rows[i], width=width_b)
            wg, wu, wd = ready
            xp, xs, ready = _ffn(xp, xs, vecs, wg, wu, wd, g_row=ffn1_row + i, cast=nxt1,
                                 name=f"ffn1_{i}")
            win, wout, wg, wu, wd = ready
            xp3, nb = _mixer_b_prompt(xp.reshape(bsz, seq, d), vecs, win, wout, **kw)
            state = jnp.transpose(state_conv_b[j], (1, 0, 2))
            xs, ns = _mixer_b_sample(xs, state, vecs, win, wout, **kw)
            xp, xs, ready = _ffn(xp3.reshape(bsz * seq, d), xs, vecs, wg, wu, wd,
                                 g_row=ffn2_row + i, final_row=fin, cast=nxt2, name=f"ffn2_{i}")
            new_b_p.append(nb)
            new_b_s.append(jnp.transpose(ns, (1, 0, 2))[None])

    return (xp.reshape(bsz, seq, d), xs.reshape(n_dec, 1, d),
            _cat(new_a_p), _cat(new_a_s), _cat(new_b_p), _cat(new_b_s))
```

```python
import functools

import jax
import jax.numpy as jnp
from jax.experimental import pallas as pl
from jax.experimental.pallas import tpu as pltpu

RMS_EPS = 1e-6
LN_EPS = 1e-5
FFN_RES_WEIGHT = 0.5

SUBLANES = 8
LANES = 128
BF16_ROWS = 16
VMEM_LIMIT_BYTES = 56 * 1024 * 1024

TOKEN_TILE = 512
FF_CHUNK = 512
STATE_ROWS = 6
CAST_STEPS = 16


def _rms(x, g):
    return x * jax.lax.rsqrt(jnp.mean(x * x, axis=-1, keepdims=True) + RMS_EPS) * g


def _layer_norm(x, g, b):
    mu = jnp.mean(x, axis=-1, keepdims=True)
    xc = x - mu
    var = jnp.mean(xc * xc, axis=-1, keepdims=True)
    return xc * jax.lax.rsqrt(var + LN_EPS) * g + b


def _dot(a, b):
    return jnp.dot(a, b, preferred_element_type=jnp.float32)


def _resident(shape):
    zeros = (0,) * len(shape)
    return pl.BlockSpec(shape, lambda *_: zeros, pipeline_mode=pl.Buffered(1))


def _params(n_grid):
    return pltpu.CompilerParams(
        dimension_semantics=("arbitrary",) * n_grid,
        vmem_limit_bytes=VMEM_LIMIT_BYTES,
    )


def _cast_kernel(*refs):
    n = len(refs) // 2
    for src, dst in zip(refs[:n], refs[n:]):
        dst[...] = src[...].astype(jnp.bfloat16)


def _cast_weights(items):
    in_specs, out_specs, out_shape = [], [], []
    for w, layer in items:
        _, rows, cols = w.shape
        assert rows % (CAST_STEPS * BF16_ROWS) == 0
        blk = rows // CAST_STEPS
        in_specs.append(pl.BlockSpec((None, blk, cols), lambda i, layer=layer: (layer, i, 0)))
        out_specs.append(pl.BlockSpec((blk, cols), lambda i: (i, 0)))
        out_shape.append(jax.ShapeDtypeStruct((rows, cols), jnp.bfloat16))
    return pl.pallas_call(
        _cast_kernel,
        grid=(CAST_STEPS,),
        in_specs=in_specs,
        out_specs=out_specs,
        out_shape=out_shape,
        compiler_params=_params(1),
        name="cast_weights",
    )(*[w for w, _ in items])


def _side_cast_specs(items, n_tiles, tile_of_step):
    in_specs, out_specs, out_shape = [], [], []
    for w, layer in items:
        _, rows, cols = w.shape
        n_blocks = n_tiles
        while rows % (n_blocks * BF16_ROWS) or n_tiles % n_blocks:
            n_blocks -= 1
        blk, per = rows // n_blocks, n_tiles // n_blocks
        in_specs.append(pl.BlockSpec(
            (None, blk, cols), lambda s, layer=layer, per=per: (layer, tile_of_step(s) // per, 0)))
        out_specs.append(pl.BlockSpec((blk, cols), lambda s, per=per: (tile_of_step(s) // per, 0)))
        out_shape.append(jax.ShapeDtypeStruct((rows, cols), jnp.bfloat16))
    return in_specs, out_specs, out_shape


def _side_cast(srcs, dsts):
    for src, dst in zip(srcs, dsts):
        dst[...] = src[...].astype(jnp.bfloat16)


def _ffn_body(x, g, wg_ref, wu_ref, wd_ref):
    d_ff = wg_ref.shape[1]
    h = _rms(x, g).astype(jnp.bfloat16)
    y = None
    for c0 in range(0, d_ff, FF_CHUNK):
        c1 = min(c0 + FF_CHUNK, d_ff)
        gate = _dot(h, wg_ref[:, c0:c1])
        up = _dot(h, wu_ref[:, c0:c1])
        a = (gate * jax.nn.sigmoid(gate) * up).astype(jnp.bfloat16)
        part = _dot(a, wd_ref[c0:c1, :])
        y = part if y is None else y + part
    return x + FFN_RES_WEIGHT * y


def _ffn_out(x, vec_ref, wg_ref, wu_ref, wd_ref, g_row, final_row):
    out = _ffn_body(x, vec_ref[g_row:g_row + 1, :], wg_ref, wu_ref, wd_ref)
    if final_row is not None:
        out = _rms(out, vec_ref[final_row:final_row + 1, :])
    return out


def _ffn_kernel(x_ref, vec_ref, wg_ref, wu_ref, wd_ref, xs_ref, *rest, g_row, final_row, n_tiles):
    n_cast = (len(rest) - 2) // 2
    srcs, (o_ref, ys_ref), dsts = rest[:n_cast], rest[n_cast:n_cast + 2], rest[n_cast + 2:]
    s = pl.program_id(0)
    ffn = functools.partial(_ffn_out, vec_ref=vec_ref, wg_ref=wg_ref, wu_ref=wu_ref, wd_ref=wd_ref,
                            g_row=g_row, final_row=final_row)

    @pl.when(s < n_tiles)
    def _():
        o_ref[...] = ffn(x_ref[...])
        _side_cast(srcs, dsts)

    @pl.when(s == n_tiles)
    def _():
        ys_ref[...] = ffn(xs_ref[...])


def _ffn(x, xs, vecs, wg, wu, wd, *, g_row, final_row=None, cast=(), name):
    m, d = x.shape
    tm = min(TOKEN_TILE, m)
    n_tiles = m // tm
    tile = lambda s: jnp.minimum(s, n_tiles - 1)
    row = pl.BlockSpec((tm, d), lambda s: (tile(s), 0))
    c_in, c_out, c_shape = _side_cast_specs(cast, n_tiles, tile)
    res = pl.pallas_call(
        functools.partial(_ffn_kernel, g_row=g_row, final_row=final_row, n_tiles=n_tiles),
        grid=(n_tiles + 1,),
        in_specs=[row, _resident(vecs.shape), _resident(wg.shape), _resident(wu.shape),
                  _resident(wd.shape), _resident(xs.shape)] + c_in,
        out_specs=[row, _resident(xs.shape)] + c_out,
        out_shape=[jax.ShapeDtypeStruct((m, d), jnp.float32),
                   jax.ShapeDtypeStruct(xs.shape, jnp.float32)] + c_shape,
        compiler_params=_params(1),
        name=name,
    )(x, vecs, wg, wu, wd, xs, *[w for w, _ in cast])
    return res[0], res[1], res[2:]


CHAIN_TILES = 8
CHAIN_XLU_HOPS = 1
LN_ROWS = 64


def _zero_after(tiles, xlu_hops):
    b = jax.lax.bitcast_convert_type(tiles, jnp.uint32)
    t = b[0]
    for i in range(1, b.shape[0]):
        t = t | b[i]
    t = (t >> 16) >> 16
    for _ in range(xlu_hops):
        t = pltpu.roll(t, 1, axis=1)
    return t


def _after(x, zero):
    if zero is None:
        return x
    return jnp.where(zero == 0, x, 0.0)


def _conv_pad(width):
    return -(-(width - 1) // SUBLANES) * SUBLANES


def _dwconv(tail_ref, keep_tail, v_ref, w_ref, w_row, width, out_ref, *, chained):
    tm, d = v_ref.shape
    pad = tail_ref.shape[0]
    first = pad - (width - 1)
    g = CHAIN_TILES
    nq = (first + width - 1) // SUBLANES + 1
    assert pad == SUBLANES * (nq - 1) and (tm // SUBLANES) % g == 0
    sub = jax.lax.broadcasted_iota(jnp.int32, (g, SUBLANES, LANES), 1)
    taps_of = [[(q, SUBLANES * q + r - first) for q in range(nq)
                if 0 <= SUBLANES * q + r - first < width] for r in range(SUBLANES)]
    zero = None
    for c0 in range(0, d, LANES):
        lanes = slice(c0, c0 + LANES)
        w = [jnp.broadcast_to(w_ref[w_row + k:w_row + k + 1, lanes], (SUBLANES, LANES))
             for k in range(width)]
        win = tail_ref[:, lanes]
        if keep_tail is not None:
            win = jnp.where(keep_tail, win, 0.0)
        win = win.reshape(nq - 1, SUBLANES, LANES)
        halo = None
        for j0 in range(0, tm // SUBLANES, g):
            new = v_ref[j0 * SUBLANES:(j0 + g) * SUBLANES, lanes].reshape(g, SUBLANES, LANES)
            if chained:
                new = jnp.concatenate([new[:g - 1], _after(new[g - 1], zero)[None]], axis=0)
            x = jnp.concatenate([win, new], axis=0)
            acc, new_halo = None, []
            for r in range(SUBLANES):
                if not taps_of[r]:
                    continue
                lo = 0 if (r == 0 or halo is None) else 1
                n = g if r == 0 else g + 1
                y = None
                for q, k in taps_of[r]:
                    term = x[q + lo:q + n] * w[k][None]
                    y = term if y is None else y + term
                if r == 0:
                    z = y
                else:
                    rot = pltpu.roll(y, SUBLANES - r, axis=1)
                    if halo is not None:
                        rot = jnp.concatenate([halo[len(new_halo)], rot], axis=0)
                    new_halo.append(rot[g:g + 1])
                    z = jnp.where(sub < SUBLANES - r, rot[0:g], rot[1:g + 1])
                acc = z if acc is None else acc + z
            out_ref[j0 * SUBLANES:(j0 + g) * SUBLANES, lanes] = acc.reshape(g * SUBLANES, LANES)
            halo = new_halo
            win = x[g:g + nq - 1]
            if chained:
                zero = _zero_after(acc, CHAIN_XLU_HOPS)
    return zero


def _ffn_glu_kernel(x_ref, vec_ref, wg_ref, wu_ref, wd_ref, w1_ref, xs_ref, *rest,
                    g_row, v0, width, tiles_per_seq, n_tiles):
    n_cast = (len(rest) - 4) // 2
    srcs, dsts = rest[:n_cast], rest[n_cast + 4:]
    x1_ref, v_ref, nb_ref, ys_ref = rest[n_cast:n_cast + 4]
    s = pl.program_id(0)
    tm, d = x_ref.shape
    vec = lambda i: vec_ref[v0 + i:v0 + i + 1, :]
    ffn = functools.partial(_ffn_out, vec_ref=vec_ref, wg_ref=wg_ref, wu_ref=wu_ref, wd_ref=wd_ref,
                            g_row=g_row, final_row=None)

    @pl.when(s < n_tiles)
    def _():
        x1 = ffn(x_ref[...])
        x1_ref[...] = x1
        hn = _rms(x1, vec(0)).astype(jnp.bfloat16)
        a = _dot(hn, w1_ref[:, 0:d]) + vec(1)
        g = _dot(hn, w1_ref[:, d:2 * d]) + vec(2)
        v_ref[...] = a * jax.nn.sigmoid(g)
        _side_cast(srcs, dsts)

        @pl.when(s % tiles_per_seq == tiles_per_seq - 1)
        def _():
            nb_ref[0, 0] = v_ref[tm - (width - 1):tm, :]

    @pl.when(s == n_tiles)
    def _():
        ys_ref[...] = ffn(xs_ref[...])


def _ffn_glu(x, xs, vecs, wg, wu, wd, w1, *, g_row, v0, width, seq, cast=()):
    m, d = x.shape
    tm = min(TOKEN_TILE, seq)
    n_tiles, tiles_per_seq = m // tm, seq // tm
    tile = lambda s: jnp.minimum(s, n_tiles - 1)
    row = pl.BlockSpec((tm, d), lambda s: (tile(s), 0))
    c_in, c_out, c_shape = _side_cast_specs(cast, n_tiles, tile)
    res = pl.pallas_call(
        functools.partial(_ffn_glu_kernel, g_row=g_row, v0=v0, width=width,
                          tiles_per_seq=tiles_per_seq, n_tiles=n_tiles),
        grid=(n_tiles + 1,),
        in_specs=[row, _resident(vecs.shape), _resident(wg.shape), _resident(wu.shape),
                  _resident(wd.shape), _resident(w1.shape), _resident(xs.shape)] + c_in,
        out_specs=[row, row,
                   pl.BlockSpec((1, 1, width - 1, d),
                                lambda s: (0, tile(s) // tiles_per_seq, 0, 0)),
                   _resident(xs.shape)] + c_out,
        out_shape=[jax.ShapeDtypeStruct((m, d), jnp.float32),
                   jax.ShapeDtypeStruct((m, d), jnp.float32),
                   jax.ShapeDtypeStruct((1, m // seq, width - 1, d), jnp.float32),
                   jax.ShapeDtypeStruct(xs.shape, jnp.float32)] + c_shape,
        compiler_params=_params(1),
        name="ffn1_glu",
    )(x, vecs, wg, wu, wd, w1, xs, *[w for w, _ in cast])
    return res[0], res[1], res[2], res[3], res[4:]


def _conv_ln_swish(tail_ref, keep_tail, v_ref, vec_ref, v0, width, conv_ref, c_ref, *, chained):
    tm, d = v_ref.shape
    pad = tail_ref.shape[0]
    vec = lambda i: vec_ref[v0 + i:v0 + i + 1, :]
    zero = _dwconv(tail_ref, keep_tail, v_ref, vec_ref, v0 + 7, width, conv_ref, chained=chained)
    for r0 in range(0, tm, LN_ROWS):
        cin = conv_ref[r0:r0 + LN_ROWS, :]
        if chained:
            corner = _after(cin[0:SUBLANES, 0:LANES], zero)
            top = jnp.concatenate([corner, cin[0:SUBLANES, LANES:]], axis=1)
            cin = jnp.concatenate([top, cin[SUBLANES:]], axis=0)
        c = _layer_norm(cin + vec(3), vec(4), vec(5))
        c = c * jax.nn.sigmoid(c)
        c_ref[r0:r0 + LN_ROWS, :] = c.astype(jnp.bfloat16)
        if chained:
            zero = _zero_after(c.reshape(LN_ROWS // SUBLANES, SUBLANES, d)[:, :, 0:LANES], 0)
    tail_ref[...] = v_ref[tm - pad:tm, :]


def _conv_ffn_kernel(x1_ref, v_ref, vec_ref, w2_ref, wg_ref, wu_ref, wd_ref, xs_ref, *rest,
                     g_row, final_row, v0, width, tiles_per_seq):
    n_cast = (len(rest) - 5) // 2
    srcs, (o_ref, ys_ref), dsts = rest[:n_cast], rest[n_cast:n_cast + 2], rest[n_cast + 2:-3]
    c_buf, tail_ref, conv_ref = rest[-3:]
    s = pl.program_id(0)
    slot = s % 2
    stage = functools.partial(_conv_ln_swish, tail_ref, v_ref=v_ref, vec_ref=vec_ref, v0=v0,
                              width=width, conv_ref=conv_ref)
    ffn = functools.partial(_ffn_out, vec_ref=vec_ref, wg_ref=wg_ref, wu_ref=wu_ref, wd_ref=wd_ref,
                            g_row=g_row, final_row=final_row)

    @pl.when(s == 0)
    def _():
        tail_ref[...] = jnp.zeros_like(tail_ref)
        stage(keep_tail=None, c_ref=c_buf.at[0], chained=False)
        ys_ref[...] = ffn(xs_ref[...])

    @pl.when(s > 0)
    def _():
        c = c_buf[1 - slot]
        x2 = x1_ref[...] + _dot(c, w2_ref[...]) + vec_ref[v0 + 6:v0 + 7, :]
        o_ref[...] = ffn(x2)
        _side_cast(srcs, dsts)
        stage(keep_tail=s % tiles_per_seq != 0, c_ref=c_buf.at[slot], chained=True)


def _conv_ffn(x1, v, xs, vecs, w2, wg, wu, wd, *, g_row, final_row, v0, width, seq, cast=()):
    m, d = x1.shape
    tm = min(TOKEN_TILE, seq)
    n_tiles = m // tm
    pad = _conv_pad(width)
    tile = lambda s: jnp.maximum(s - 1, 0)
    behind = pl.BlockSpec((tm, d), lambda s: (tile(s), 0))
    ahead = pl.BlockSpec((tm, d), lambda s: (jnp.minimum(s, n_tiles - 1), 0))
    c_in, c_out, c_shape = _side_cast_specs(cast, n_tiles, tile)
    res = pl.pallas_call(
        functools.partial(_conv_ffn_kernel, g_row=g_row, final_row=final_row, v0=v0, width=width,
                          tiles_per_seq=seq // tm),
        grid=(n_tiles + 1,),
        in_specs=[behind, ahead, _resident(vecs.shape), _resident(w2.shape), _resident(wg.shape),
                  _resident(wu.shape), _resident(wd.shape), _resident(xs.shape)] + c_in,
        out_specs=[behind, _resident(xs.shape)] + c_out,
        out_shape=[jax.ShapeDtypeStruct((m, d), jnp.float32),
                   jax.ShapeDtypeStruct(xs.shape, jnp.float32)] + c_shape,
        scratch_shapes=[pltpu.VMEM((2, tm, d), jnp.bfloat16), pltpu.VMEM((pad, d), jnp.float32),
                        pltpu.VMEM((tm, d), jnp.float32)],
        compiler_params=_params(1),
        name="conv_ffn2",
    )(x1, v, vecs, w2, wg, wu, wd, xs, *[w for w, _ in cast])
    return res[0], res[1], res[2:]


def _ffn_mixer_b_kernel(x_ref, vec_ref, wg_ref, wu_ref, wd_ref, win_ref, wout_ref, xs_ref, *rest,
                        g_row, v0, width, tiles_per_seq, n_tiles):
    n_cast = (len(rest) - 6) // 2
    srcs, dsts = rest[:n_cast], rest[n_cast + 3:-3]
    o_ref, nb_ref, ys_ref = rest[n_cast:n_cast + 3]
    u_ref, tail_ref, conv_ref = rest[-3:]
    s = pl.program_id(0)
    tm, d = x_ref.shape
    pad = tail_ref.shape[0]
    ffn = functools.partial(_ffn_out, vec_ref=vec_ref, wg_ref=wg_ref, wu_ref=wu_ref, wd_ref=wd_ref,
                            g_row=g_row, final_row=None)

    @pl.when(s == 0)
    def _():
        tail_ref[...] = jnp.zeros_like(tail_ref)

    @pl.when(s < n_tiles)
    def _():
        x1 = ffn(x_ref[...])
        hn = _rms(x1, vec_ref[v0:v0 + 1, :]).astype(jnp.bfloat16)
        b_gate = _dot(hn, win_ref[:, 0:d])
        c_gate = _dot(hn, win_ref[:, d:2 * d])
        h = _dot(hn, win_ref[:, 2 * d:3 * d])
        u_ref[...] = c_gate * h
        _dwconv(tail_ref, s % tiles_per_seq != 0, u_ref, vec_ref, v0 + 1, width, conv_ref,
                chained=False)
        y = (b_gate * conv_ref[...]).astype(jnp.bfloat16)
        o_ref[...] = x1 + _dot(y, wout_ref[...])
        tail_ref[...] = u_ref[tm - pad:tm, :]
        _side_cast(srcs, dsts)

        @pl.when(s % tiles_per_seq == tiles_per_seq - 1)
        def _():
            nb_ref[0, 0] = u_ref[tm - (width - 1):tm, :]

    @pl.when(s == n_tiles)
    def _():
        ys_ref[...] = ffn(xs_ref[...])


def _ffn_mixer_b(x, xs, vecs, wg, wu, wd, win, wout, *, g_row, v0, width, seq, cast=()):
    m, d = x.shape
    tm = min(TOKEN_TILE, seq)
    n_tiles, tiles_per_seq = m // tm, seq // tm
    pad = _conv_pad(width)
    tile = lambda s: jnp.minimum(s, n_tiles - 1)
    row = pl.BlockSpec((tm, d), lambda s: (tile(s), 0))
    c_in, c_out, c_shape = _side_cast_specs(cast, n_tiles, tile)
    res = pl.pallas_call(
        functools.partial(_ffn_mixer_b_kernel, g_row=g_row, v0=v0, width=width,
                          tiles_per_seq=tiles_per_seq, n_tiles=n_tiles),
        grid=(n_tiles + 1,),
        in_specs=[row, _resident(vecs.shape), _resident(wg.shape), _resident(wu.shape),
                  _resident(wd.shape), _resident(win.shape), _resident(wout.shape),
                  _resident(xs.shape)] + c_in,
        out_specs=[row,
                   pl.BlockSpec((1, 1, width - 1, d),
                                lambda s: (0, tile(s) // tiles_per_seq, 0, 0)),
                   _resident(xs.shape)] + c_out,
        out_shape=[jax.ShapeDtypeStruct((m, d), jnp.float32),
                   jax.ShapeDtypeStruct((1, m // seq, width - 1, d), jnp.float32),
                   jax.ShapeDtypeStruct(xs.shape, jnp.float32)] + c_shape,
        scratch_shapes=[pltpu.VMEM((tm, d), jnp.float32), pltpu.VMEM((pad, d), jnp.float32),
                        pltpu.VMEM((tm, d), jnp.float32)],
        compiler_params=_params(1),
        name="ffn1_mixer_b",
    )(x, vecs, wg, wu, wd, win, wout, xs, *[w for w, _ in cast])
    return res[0], res[1], res[2], res[3:]


def _mixer_a_sample_kernel(x_ref, st_ref, nxt_ref, vec_ref, w1_ref, w2_ref, o_ref, ns_ref, acc_ref,
                           *, v0, width):
    k = pl.program_id(0)
    last = pl.num_programs(0) - 1
    d = x_ref.shape[-1]
    rows = st_ref.shape[0]
    vec = lambda i: vec_ref[v0 + i:v0 + i + 1, :]

    @pl.when(k == 0)
    def _():
        acc_ref[...] = jnp.zeros_like(acc_ref)

    part = None
    for i in range(rows):
        term = st_ref[i] * vec_ref[pl.ds(v0 + 7 + rows * k + i, 1), :]
        part = term if part is None else part + term
    acc_ref[...] += part
    ns_ref[0:rows - 1] = st_ref[1:rows]

    @pl.when(k < last)
    def _():
        ns_ref[rows - 1] = nxt_ref[...]

    @pl.when(k == last)
    def _():
        x = x_ref[...]
        hn = _rms(x, vec(0)).astype(jnp.bfloat16)
        a = _dot(hn, w1_ref[:, 0:d]) + vec(1)
        g = _dot(hn, w1_ref[:, d:2 * d]) + vec(2)
        v = a * jax.nn.sigmoid(g)
        ns_ref[rows - 1] = v
        c = _layer_norm(acc_ref[...] + v * vec(7 + width - 1) + vec(3), vec(4), vec(5))
        c = (c * jax.nn.sigmoid(c)).astype(jnp.bfloat16)
        o_ref[...] = x + _dot(c, w2_ref[...]) + vec(6)


def _mixer_a_sample(x, state, vecs, w1, w2, *, width, v0):
    n, d = x.shape
    hist = width - 1
    assert hist % STATE_ROWS == 0
    blocks = pl.BlockSpec((STATE_ROWS, n, d), lambda k: (k, 0, 0))
    nxt = pl.BlockSpec((None, n, d), lambda k: (jnp.minimum(STATE_ROWS * (k + 1), hist - 1), 0, 0))
    return pl.pallas_call(
        functools.partial(_mixer_a_sample_kernel, v0=v0, width=width),
        grid=(hist // STATE_ROWS,),
        in_specs=[_resident((n, d)), blocks, nxt, _resident(vecs.shape), _resident(w1.shape),
                  _resident(w2.shape)],
        out_specs=[pl.BlockSpec((n, d), lambda k: (0, 0)), blocks],
        out_shape=[jax.ShapeDtypeStruct((n, d), jnp.float32),
                   jax.ShapeDtypeStruct((hist, n, d), jnp.float32)],
        scratch_shapes=[pltpu.VMEM((n, d), jnp.float32)],
        compiler_params=_params(1),
        name="mixer_a_sample",
    )(x, state, state, vecs, w1, w2)


def _mixer_b_sample_kernel(x_ref, st_ref, vec_ref, win_ref, wout_ref, o_ref, ns_ref, *, v0):
    d = x_ref.shape[-1]
    hist = st_ref.shape[0]
    tap = lambda k: vec_ref[v0 + 1 + k:v0 + 2 + k, :]
    x = x_ref[...]
    hn = _rms(x, vec_ref[v0:v0 + 1, :]).astype(jnp.bfloat16)
    b_gate = _dot(hn, win_ref[:, 0:d])
    c_gate = _dot(hn, win_ref[:, d:2 * d])
    h = _dot(hn, win_ref[:, 2 * d:3 * d])
    u = c_gate * h
    conv = st_ref[0] * tap(0)
    for k in range(1, hist):
        conv = conv + st_ref[k] * tap(k)
    conv = conv + u * tap(hist)
    ns_ref[0:hist - 1] = st_ref[1:hist]
    ns_ref[hist - 1] = u
    y = (b_gate * conv).astype(jnp.bfloat16)
    o_ref[...] = x + _dot(y, wout_ref[...])


def _mixer_b_sample(x, state, vecs, win, wout, *, width, v0):
    n, d = x.shape
    return pl.pallas_call(
        functools.partial(_mixer_b_sample_kernel, v0=v0),
        out_shape=[jax.ShapeDtypeStruct((n, d), jnp.float32),
                   jax.ShapeDtypeStruct((width - 1, n, d), jnp.float32)],
        compiler_params=pltpu.CompilerParams(vmem_limit_bytes=VMEM_LIMIT_BYTES),
        name="mixer_b_sample",
    )(x, state, vecs, win, wout)


def _pack_vectors(d, groups):
    rows, starts, n = [], [], 0
    for v in groups:
        v = v.reshape(-1, d)
        starts.append(n)
        rows.append(v)
        n += v.shape[0]
    fill = -n % SUBLANES
    if fill:
        rows.append(jnp.zeros((fill, d), jnp.float32))
    return jnp.concatenate(rows, axis=0), starts


def _cat(per_layer):
    return per_layer[0] if len(per_layer) == 1 else jnp.concatenate(per_layer, axis=0)


def kernel(x_prompt, x_sample, state_conv_a, state_conv_b, ffn1_norm, ffn1_w_gate, ffn1_w_up, ffn1_w_down, mix_norm, ffn2_norm, ffn2_w_gate, ffn2_w_up, ffn2_w_down, final_norm, a_w_pw1, a_b_pw1, a_w_dw, a_b_dw, a_ln_g, a_ln_b, a_w_pw2, a_b_pw2, b_w_in, b_w_conv, b_w_out):
    depth = ffn1_norm.shape[0]
    bsz, seq, d = x_prompt.shape
    n_dec = x_sample.shape[0]
    width_a = a_w_dw.shape[1]
    width_b = b_w_conv.shape[1]

    groups = [ffn1_norm, ffn2_norm, final_norm]
    for i in range(depth):
        j = i // 2
        if i % 2 == 0:
            groups += [mix_norm[i], a_b_pw1[j], a_b_dw[j], a_ln_g[j], a_ln_b[j], a_b_pw2[j], a_w_dw[j]]
        else:
            groups += [mix_norm[i], b_w_conv[j]]
    vecs, starts = _pack_vectors(d, groups)
    ffn1_row, ffn2_row, final_row = starts[0], starts[1], starts[2]
    mix_rows, p = [], 3
    for i in range(depth):
        mix_rows.append(starts[p])
        p += 7 if i % 2 == 0 else 2

    def ffn1_src(i):
        return [(ffn1_w_gate, i), (ffn1_w_up, i), (ffn1_w_down, i)]

    def ffn2_src(i):
        return [(ffn2_w_gate, i), (ffn2_w_up, i), (ffn2_w_down, i)]

    stage_src = []
    for i in range(depth):
        j = i // 2
        if i % 2 == 0:
            stage_src += [ffn1_src(i) + [(a_w_pw1, j)], [(a_w_pw2, j)] + ffn2_src(i)]
        else:
            stage_src += [ffn1_src(i) + [(b_w_in, j), (b_w_out, j)], ffn2_src(i)]
    stage_src.append([])
    ready = list(_cast_weights(stage_src[0]))

    xp = x_prompt.reshape(bsz * seq, d)
    xs = x_sample.reshape(n_dec, d)
    new_a_p, new_a_s, new_b_p, new_b_s = [], [], [], []
    for i in range(depth):
        j = i // 2
        fin = final_row if i == depth - 1 else None
        nxt1, nxt2 = stage_src[2 * i + 1], stage_src[2 * i + 2]
        if i % 2 == 0:
            kw = dict(v0=mix_rows[i], width=width_a)
            wg, wu, wd, w1 = ready
            x1, v, nb, xs, ready = _ffn_glu(xp, xs, vecs, wg, wu, wd, w1, g_row=ffn1_row + i,
                                            seq=seq, cast=nxt1, **kw)
            w2, wg, wu, wd = ready
            state = jnp.transpose(state_conv_a[j], (1, 0, 2))
            xs, ns = _mixer_a_sample(xs, state, vecs, w1, w2, **kw)
            xp, xs, ready = _conv_ffn(x1, v, xs, vecs, w2, wg, wu, wd, g_row=ffn2_row + i,
                                      final_row=fin, seq=seq, cast=nxt2, **kw)
            new_a_p.append(nb)
            new_a_s.append(jnp.transpose(ns, (1, 0, 2))[None])
        else:
            kw = dict(v0=mix_rows[i], width=width_b)
            wg, wu, wd, win, wout = ready
            xp, nb, xs, ready = _ffn_mixer_b(xp, xs, vecs, wg, wu, wd, win, wout,
                                             g_row=ffn1_row + i, seq=seq, cast=nxt1, **kw)
            wg, wu, wd = ready
            state = jnp.transpose(state_conv_b[j], (1, 0, 2))
            xs, ns = _mixer_b_sample(xs, state, vecs, win, wout, **kw)
            xp, xs, ready = _ffn(xp, xs, vecs, wg, wu, wd, g_row=ffn2_row + i, final_row=fin,
                                 cast=nxt2, name=f"ffn2_{i}")
            new_b_p.append(nb)
            new_b_s.append(jnp.transpose(ns, (1, 0, 2))[None])

    return (xp.reshape(bsz, seq, d), xs.reshape(n_dec, 1, d),
            _cat(new_a_p), _cat(new_a_s), _cat(new_b_p), _cat(new_b_s))
```

```python
import functools

import jax
import jax.numpy as jnp
from jax.experimental import pallas as pl
from jax.experimental.pallas import tpu as pltpu

RMS_EPS = 1e-6
LN_EPS = 1e-5
FFN_RES_WEIGHT = 0.5

SUBLANES = 8
LANES = 128
BF16_ROWS = 16
VMEM_LIMIT_BYTES = 56 * 1024 * 1024

TOKEN_TILE = 512
FF_CHUNK = 512
STATE_ROWS = 6
CAST_STEPS = 16


def _rms(x, g):
    return x * jax.lax.rsqrt(jnp.mean(x * x, axis=-1, keepdims=True) + RMS_EPS) * g


def _layer_norm(x, g, b):
    mu = jnp.mean(x, axis=-1, keepdims=True)
    xc = x - mu
    var = jnp.mean(xc * xc, axis=-1, keepdims=True)
    return xc * jax.lax.rsqrt(var + LN_EPS) * g + b


def _dot(a, b):
    return jnp.dot(a, b, preferred_element_type=jnp.float32)


def _resident(shape):
    zeros = (0,) * len(shape)
    return pl.BlockSpec(shape, lambda *_: zeros, pipeline_mode=pl.Buffered(1))


def _params(n_grid):
    return pltpu.CompilerParams(
        dimension_semantics=("arbitrary",) * n_grid,
        vmem_limit_bytes=VMEM_LIMIT_BYTES,
    )


def _cast_kernel(*refs):
    n = len(refs) // 2
    for src, dst in zip(refs[:n], refs[n:]):
        dst[...] = src[...].astype(jnp.bfloat16)


def _cast_weights(items):
    in_specs, out_specs, out_shape = [], [], []
    for w, layer in items:
        _, rows, cols = w.shape
        assert rows % (CAST_STEPS * BF16_ROWS) == 0
        blk = rows // CAST_STEPS
        in_specs.append(pl.BlockSpec((None, blk, cols), lambda i, layer=layer: (layer, i, 0)))
        out_specs.append(pl.BlockSpec((blk, cols), lambda i: (i, 0)))
        out_shape.append(jax.ShapeDtypeStruct((rows, cols), jnp.bfloat16))
    return pl.pallas_call(
        _cast_kernel,
        grid=(CAST_STEPS,),
        in_specs=in_specs,
        out_specs=out_specs,
        out_shape=out_shape,
        compiler_params=_params(1),
        name="cast_weights",
    )(*[w for w, _ in items])


def _side_cast_specs(items, n_tiles, tile_of_step):
    in_specs, out_specs, out_shape = [], [], []
    for w, layer in items:
        _, rows, cols = w.shape
        n_blocks = n_tiles
        while rows % (n_blocks * BF16_ROWS) or n_tiles % n_blocks:
            n_blocks -= 1
        blk, per = rows // n_blocks, n_tiles // n_blocks
        in_specs.append(pl.BlockSpec(
            (None, blk, cols), lambda s, layer=layer, per=per: (layer, tile_of_step(s) // per, 0)))
        out_specs.append(pl.BlockSpec((blk, cols), lambda s, per=per: (tile_of_step(s) // per, 0)))
        out_shape.append(jax.ShapeDtypeStruct((rows, cols), jnp.bfloat16))
    return in_specs, out_specs, out_shape


def _side_cast(srcs, dsts):
    for src, dst in zip(srcs, dsts):
        dst[...] = src[...].astype(jnp.bfloat16)


def _ffn_body(x, g, wg_ref, wu_ref, wd_ref):
    d_ff = wg_ref.shape[1]
    h = _rms(x, g).astype(jnp.bfloat16)
    y = None
    for c0 in range(0, d_ff, FF_CHUNK):
        c1 = min(c0 + FF_CHUNK, d_ff)
        gate = _dot(h, wg_ref[:, c0:c1])
        up = _dot(h, wu_ref[:, c0:c1])
        a = (gate * jax.nn.sigmoid(gate) * up).astype(jnp.bfloat16)
        part = _dot(a, wd_ref[c0:c1, :])
        y = part if y is None else y + part
    return x + FFN_RES_WEIGHT * y


def _ffn_out(x, vec_ref, wg_ref, wu_ref, wd_ref, g_row, final_row):
    out = _ffn_body(x, vec_ref[g_row:g_row + 1, :], wg_ref, wu_ref, wd_ref)
    if final_row is not None:
        out = _rms(out, vec_ref[final_row:final_row + 1, :])
    return out


def _ffn_kernel(x_ref, vec_ref, wg_ref, wu_ref, wd_ref, xs_ref, *rest, g_row, final_row, n_tiles):
    n_cast = (len(rest) - 2) // 2
    srcs, (o_ref, ys_ref), dsts = rest[:n_cast], rest[n_cast:n_cast + 2], rest[n_cast + 2:]
    s = pl.program_id(0)
    ffn = functools.partial(_ffn_out, vec_ref=vec_ref, wg_ref=wg_ref, wu_ref=wu_ref, wd_ref=wd_ref,
                            g_row=g_row, final_row=final_row)

    @pl.when(s < n_tiles)
    def _():
        o_ref[...] = ffn(x_ref[...])
        _side_cast(srcs, dsts)

    @pl.when(s == n_tiles)
    def _():
        ys_ref[...] = ffn(xs_ref[...])


def _ffn(x, xs, vecs, wg, wu, wd, *, g_row, final_row=None, cast=(), name):
    m, d = x.shape
    tm = min(TOKEN_TILE, m)
    n_tiles = m // tm
    tile = lambda s: jnp.minimum(s, n_tiles - 1)
    row = pl.BlockSpec((tm, d), lambda s: (tile(s), 0))
    c_in, c_out, c_shape = _side_cast_specs(cast, n_tiles, tile)
    res = pl.pallas_call(
        functools.partial(_ffn_kernel, g_row=g_row, final_row=final_row, n_tiles=n_tiles),
        grid=(n_tiles + 1,),
        in_specs=[row, _resident(vecs.shape), _resident(wg.shape), _resident(wu.shape),
                  _resident(wd.shape), _resident(xs.shape)] + c_in,
        out_specs=[row, _resident(xs.shape)] + c_out,
        out_shape=[jax.ShapeDtypeStruct((m, d), jnp.float32),
                   jax.ShapeDtypeStruct(xs.shape, jnp.float32)] + c_shape,
        compiler_params=_params(1),
        name=name,
    )(x, vecs, wg, wu, wd, xs, *[w for w, _ in cast])
    return res[0], res[1], res[2:]


CHAIN_TILES = 8
CHAIN_XLU_HOPS = 1
LN_ROWS = 32


def _zero_after(tiles, xlu_hops):
    b = jax.lax.bitcast_convert_type(tiles, jnp.uint32)
    t = b[0]
    for i in range(1, b.shape[0]):
        t = t | b[i]
    t = (t >> 16) >> 16
    for _ in range(xlu_hops):
        t = pltpu.roll(t, 1, axis=1)
    return t


def _after(x, zero):
    if zero is None:
        return x
    return jnp.where(zero == 0, x, 0.0)


def _conv_pad(width):
    return -(-(width - 1) // SUBLANES) * SUBLANES


def _dwconv(tail_ref, keep_tail, v_ref, w_ref, w_row, width, out_ref, *, chained):
    tm, d = v_ref.shape
    pad = tail_ref.shape[0]
    first = pad - (width - 1)
    g = CHAIN_TILES
    nq = (first + width - 1) // SUBLANES + 1
    assert pad == SUBLANES * (nq - 1) and (tm // SUBLANES) % g == 0
    sub = jax.lax.broadcasted_iota(jnp.int32, (g, SUBLANES, LANES), 1)
    taps_of = [[(q, SUBLANES * q + r - first) for q in range(nq)
                if 0 <= SUBLANES * q + r - first < width] for r in range(SUBLANES)]
    zero = None
    for c0 in range(0, d, LANES):
        lanes = slice(c0, c0 + LANES)
        w = [jnp.broadcast_to(w_ref[w_row + k:w_row + k + 1, lanes], (SUBLANES, LANES))
             for k in range(width)]
        win = tail_ref[:, lanes]
        if keep_tail is not None:
            win = jnp.where(keep_tail, win, 0.0)
        win = win.reshape(nq - 1, SUBLANES, LANES)
        halo = None
        for j0 in range(0, tm // SUBLANES, g):
            new = v_ref[j0 * SUBLANES:(j0 + g) * SUBLANES, lanes].reshape(g, SUBLANES, LANES)
            if chained:
                new = jnp.concatenate([new[:g - 1], _after(new[g - 1], zero)[None]], axis=0)
            x = jnp.concatenate([win, new], axis=0)
            acc, new_halo = None, []
            for r in range(SUBLANES):
                if not taps_of[r]:
                    continue
                lo = 0 if (r == 0 or halo is None) else 1
                n = g if r == 0 else g + 1
                y = None
                for q, k in taps_of[r]:
                    term = x[q + lo:q + n] * w[k][None]
                    y = term if y is None else y + term
                if r == 0:
                    z = y
                else:
                    rot = pltpu.roll(y, SUBLANES - r, axis=1)
                    if halo is not None:
                        rot = jnp.concatenate([halo[len(new_halo)], rot], axis=0)
                    new_halo.append(rot[g:g + 1])
                    z = jnp.where(sub < SUBLANES - r, rot[0:g], rot[1:g + 1])
                acc = z if acc is None else acc + z
            out_ref[j0 * SUBLANES:(j0 + g) * SUBLANES, lanes] = acc.reshape(g * SUBLANES, LANES)
            halo = new_halo
            win = x[g:g + nq - 1]
            if chained:
                zero = _zero_after(acc, CHAIN_XLU_HOPS)
    return zero


def _ffn_glu_kernel(x_ref, vec_ref, wg_ref, wu_ref, wd_ref, w1_ref, xs_ref, *rest,
                    g_row, v0, width, tiles_per_seq, n_tiles):
    n_cast = (len(rest) - 4) // 2
    srcs, dsts = rest[:n_cast], rest[n_cast + 4:]
    x1_ref, v_ref, nb_ref, ys_ref = rest[n_cast:n_cast + 4]
    s = pl.program_id(0)
    tm, d = x_ref.shape
    vec = lambda i: vec_ref[v0 + i:v0 + i + 1, :]
    ffn = functools.partial(_ffn_out, vec_ref=vec_ref, wg_ref=wg_ref, wu_ref=wu_ref, wd_ref=wd_ref,
                            g_row=g_row, final_row=None)

    @pl.when(s < n_tiles)
    def _():
        x1 = ffn(x_ref[...])
        x1_ref[...] = x1
        hn = _rms(x1, vec(0)).astype(jnp.bfloat16)
        a = _dot(hn, w1_ref[:, 0:d]) + vec(1)
        g = _dot(hn, w1_ref[:, d:2 * d]) + vec(2)
        v_ref[...] = a * jax.nn.sigmoid(g)
        _side_cast(srcs, dsts)

        @pl.when(s % tiles_per_seq == tiles_per_seq - 1)
        def _():
            nb_ref[0, 0] = v_ref[tm - (width - 1):tm, :]

    @pl.when(s == n_tiles)
    def _():
        ys_ref[...] = ffn(xs_ref[...])


def _ffn_glu(x, xs, vecs, wg, wu, wd, w1, *, g_row, v0, width, seq, cast=()):
    m, d = x.shape
    tm = min(TOKEN_TILE, seq)
    n_tiles, tiles_per_seq = m // tm, seq // tm
    tile = lambda s: jnp.minimum(s, n_tiles - 1)
    row = pl.BlockSpec((tm, d), lambda s: (tile(s), 0))
    c_in, c_out, c_shape = _side_cast_specs(cast, n_tiles, tile)
    res = pl.pallas_call(
        functools.partial(_ffn_glu_kernel, g_row=g_row, v0=v0, width=width,
                          tiles_per_seq=tiles_per_seq, n_tiles=n_tiles),
        grid=(n_tiles + 1,),
        in_specs=[row, _resident(vecs.shape), _resident(wg.shape), _resident(wu.shape),
                  _resident(wd.shape), _resident(w1.shape), _resident(xs.shape)] + c_in,
        out_specs=[row, row,
                   pl.BlockSpec((1, 1, width - 1, d),
                                lambda s: (0, tile(s) // tiles_per_seq, 0, 0)),
                   _resident(xs.shape)] + c_out,
        out_shape=[jax.ShapeDtypeStruct((m, d), jnp.float32),
                   jax.ShapeDtypeStruct((m, d), jnp.float32),
                   jax.ShapeDtypeStruct((1, m // seq, width - 1, d), jnp.float32),
                   jax.ShapeDtypeStruct(xs.shape, jnp.float32)] + c_shape,
        compiler_params=_params(1),
        name="ffn1_glu",
    )(x, vecs, wg, wu, wd, w1, xs, *[w for w, _ in cast])
    return res[0], res[1], res[2], res[3], res[4:]


def _conv_ln_swish(tail_ref, keep_tail, v_ref, vec_ref, v0, width, conv_ref, c_ref, *, chained):
    tm, d = v_ref.shape
    pad = tail_ref.shape[0]
    vec = lambda i: vec_ref[v0 + i:v0 + i + 1, :]
    zero = _dwconv(tail_ref, keep_tail, v_ref, vec_ref, v0 + 7, width, conv_ref, chained=chained)
    for r0 in range(0, tm, LN_ROWS):
        cin = conv_ref[r0:r0 + LN_ROWS, :]
        if chained:
            corner = _after(cin[0:SUBLANES, 0:LANES], zero)
            top = jnp.concatenate([corner, cin[0:SUBLANES, LANES:]], axis=1)
            cin = jnp.concatenate([top, cin[SUBLANES:]], axis=0)
        c = _layer_norm(cin + vec(3), vec(4), vec(5))
        c = c * jax.nn.sigmoid(c)
        c_ref[r0:r0 + LN_ROWS, :] = c.astype(jnp.bfloat16)
        if chained:
            zero = _zero_after(c.reshape(LN_ROWS // SUBLANES, SUBLANES, d)[:, :, 0:LANES], 0)
    tail_ref[...] = v_ref[tm - pad:tm, :]


def _conv_ffn_kernel(x1_ref, v_ref, vec_ref, w2_ref, wg_ref, wu_ref, wd_ref, xs_ref, *rest,
                     g_row, final_row, v0, width, tiles_per_seq):
    n_cast = (len(rest) - 5) // 2
    srcs, (o_ref, ys_ref), dsts = rest[:n_cast], rest[n_cast:n_cast + 2], rest[n_cast + 2:-3]
    c_buf, tail_ref, conv_ref = rest[-3:]
    s = pl.program_id(0)
    slot = s % 2
    stage = functools.partial(_conv_ln_swish, tail_ref, v_ref=v_ref, vec_ref=vec_ref, v0=v0,
                              width=width, conv_ref=conv_ref)
    ffn = functools.partial(_ffn_out, vec_ref=vec_ref, wg_ref=wg_ref, wu_ref=wu_ref, wd_ref=wd_ref,
                            g_row=g_row, final_row=final_row)

    @pl.when(s == 0)
    def _():
        tail_ref[...] = jnp.zeros_like(tail_ref)
        stage(keep_tail=None, c_ref=c_buf.at[0], chained=False)
        ys_ref[...] = ffn(xs_ref[...])

    @pl.when(s > 0)
    def _():
        c = c_buf[1 - slot]
        x2 = x1_ref[...] + _dot(c, w2_ref[...]) + vec_ref[v0 + 6:v0 + 7, :]
        o_ref[...] = ffn(x2)
        _side_cast(srcs, dsts)
        stage(keep_tail=s % tiles_per_seq != 0, c_ref=c_buf.at[slot], chained=True)


def _conv_ffn(x1, v, xs, vecs, w2, wg, wu, wd, *, g_row, final_row, v0, width, seq, cast=()):
    m, d = x1.shape
    tm = min(TOKEN_TILE, seq)
    n_tiles = m // tm
    pad = _conv_pad(width)
    tile = lambda s: jnp.maximum(s - 1, 0)
    behind = pl.BlockSpec((tm, d), lambda s: (tile(s), 0))
    ahead = pl.BlockSpec((tm, d), lambda s: (jnp.minimum(s, n_tiles - 1), 0))
    c_in, c_out, c_shape = _side_cast_specs(cast, n_tiles, tile)
    res = pl.pallas_call(
        functools.partial(_conv_ffn_kernel, g_row=g_row, final_row=final_row, v0=v0, width=width,
                          tiles_per_seq=seq // tm),
        grid=(n_tiles + 1,),
        in_specs=[behind, ahead, _resident(vecs.shape), _resident(w2.shape), _resident(wg.shape),
                  _resident(wu.shape), _resident(wd.shape), _resident(xs.shape)] + c_in,
        out_specs=[behind, _resident(xs.shape)] + c_out,
        out_shape=[jax.ShapeDtypeStruct((m, d), jnp.float32),
                   jax.ShapeDtypeStruct(xs.shape, jnp.float32)] + c_shape,
        scratch_shapes=[pltpu.VMEM((2, tm, d), jnp.bfloat16), pltpu.VMEM((pad, d), jnp.float32),
                        pltpu.VMEM((tm, d), jnp.float32)],
        compiler_params=_params(1),
        name="conv_ffn2",
    )(x1, v, vecs, w2, wg, wu, wd, xs, *[w for w, _ in cast])
    return res[0], res[1], res[2:]


def _ffn_mixer_b_kernel(x_ref, vec_ref, wg_ref, wu_ref, wd_ref, win_ref, wout_ref, xs_ref, *rest,
                        g_row, v0, width, tiles_per_seq, n_tiles):
    n_cast = (len(rest) - 6) // 2
    srcs, dsts = rest[:n_cast], rest[n_cast + 3:-3]
    o_ref, nb_ref, ys_ref = rest[n_cast:n_cast + 3]
    u_ref, tail_ref, conv_ref = rest[-3:]
    s = pl.program_id(0)
    tm, d = x_ref.shape
    pad = tail_ref.shape[0]
    ffn = functools.partial(_ffn_out, vec_ref=vec_ref, wg_ref=wg_ref, wu_ref=wu_ref, wd_ref=wd_ref,
                            g_row=g_row, final_row=None)

    @pl.when(s == 0)
    def _():
        tail_ref[...] = jnp.zeros_like(tail_ref)

    @pl.when(s < n_tiles)
    def _():
        x1 = ffn(x_ref[...])
        hn = _rms(x1, vec_ref[v0:v0 + 1, :]).astype(jnp.bfloat16)
        b_gate = _dot(hn, win_ref[:, 0:d])
        c_gate = _dot(hn, win_ref[:, d:2 * d])
        h = _dot(hn, win_ref[:, 2 * d:3 * d])
        u_ref[...] = c_gate * h
        _dwconv(tail_ref, s % tiles_per_seq != 0, u_ref, vec_ref, v0 + 1, width, conv_ref,
                chained=False)
        y = (b_gate * conv_ref[...]).astype(jnp.bfloat16)
        o_ref[...] = x1 + _dot(y, wout_ref[...])
        tail_ref[...] = u_ref[tm - pad:tm, :]
        _side_cast(srcs, dsts)

        @pl.when(s % tiles_per_seq == tiles_per_seq - 1)
        def _():
            nb_ref[0, 0] = u_ref[tm - (width - 1):tm, :]

    @pl.when(s == n_tiles)
    def _():
        ys_ref[...] = ffn(xs_ref[...])


def _ffn_mixer_b(x, xs, vecs, wg, wu, wd, win, wout, *, g_row, v0, width, seq, cast=()):
    m, d = x.shape
    tm = min(TOKEN_TILE, seq)
    n_tiles, tiles_per_seq = m // tm, seq // tm
    pad = _conv_pad(width)
    tile = lambda s: jnp.minimum(s, n_tiles - 1)
    row = pl.BlockSpec((tm, d), lambda s: (tile(s), 0))
    c_in, c_out, c_shape = _side_cast_specs(cast, n_tiles, tile)
    res = pl.pallas_call(
        functools.partial(_ffn_mixer_b_kernel, g_row=g_row, v0=v0, width=width,
                          tiles_per_seq=tiles_per_seq, n_tiles=n_tiles),
        grid=(n_tiles + 1,),
        in_specs=[row, _resident(vecs.shape), _resident(wg.shape), _resident(wu.shape),
                  _resident(wd.shape), _resident(win.shape), _resident(wout.shape),
                  _resident(xs.shape)] + c_in,
        out_specs=[row,
                   pl.BlockSpec((1, 1, width - 1, d),
                                lambda s: (0, tile(s) // tiles_per_seq, 0, 0)),
                   _resident(xs.shape)] + c_out,
        out_shape=[jax.ShapeDtypeStruct((m, d), jnp.float32),
                   jax.ShapeDtypeStruct((1, m // seq, width - 1, d), jnp.float32),
                   jax.ShapeDtypeStruct(xs.shape, jnp.float32)] + c_shape,
        scratch_shapes=[pltpu.VMEM((tm, d), jnp.float32), pltpu.VMEM((pad, d), jnp.float32),
                        pltpu.VMEM((tm, d), jnp.float32)],
        compiler_params=_params(1),
        name="ffn1_mixer_b",
    )(x, vecs, wg, wu, wd, win, wout, xs, *[w for w, _ in cast])
    return res[0], res[1], res[2], res[3:]


def _mixer_a_sample_kernel(x_ref, st_ref, nxt_ref, vec_ref, w1_ref, w2_ref, o_ref, ns_ref, acc_ref,
                           *, v0, width):
    k = pl.program_id(0)
    last = pl.num_programs(0) - 1
    d = x_ref.shape[-1]
    rows = st_ref.shape[0]
    vec = lambda i: vec_ref[v0 + i:v0 + i + 1, :]

    @pl.when(k == 0)
    def _():
        acc_ref[...] = jnp.zeros_like(acc_ref)

    part = None
    for i in range(rows):
        term = st_ref[i] * vec_ref[pl.ds(v0 + 7 + rows * k + i, 1), :]
        part = term if part is None else part + term
    acc_ref[...] += part
    ns_ref[0:rows - 1] = st_ref[1:rows]

    @pl.when(k < last)
    def _():
        ns_ref[rows - 1] = nxt_ref[...]

    @pl.when(k == last)
    def _():
        x = x_ref[...]
        hn = _rms(x, vec(0)).astype(jnp.bfloat16)
        a = _dot(hn, w1_ref[:, 0:d]) + vec(1)
        g = _dot(hn, w1_ref[:, d:2 * d]) + vec(2)
        v = a * jax.nn.sigmoid(g)
        ns_ref[rows - 1] = v
        c = _layer_norm(acc_ref[...] + v * vec(7 + width - 1) + vec(3), vec(4), vec(5))
        c = (c * jax.nn.sigmoid(c)).astype(jnp.bfloat16)
        o_ref[...] = x + _dot(c, w2_ref[...]) + vec(6)


def _mixer_a_sample(x, state, vecs, w1, w2, *, width, v0):
    n, d = x.shape
    hist = width - 1
    assert hist % STATE_ROWS == 0
    blocks = pl.BlockSpec((STATE_ROWS, n, d), lambda k: (k, 0, 0))
    nxt = pl.BlockSpec((None, n, d), lambda k: (jnp.minimum(STATE_ROWS * (k + 1), hist - 1), 0, 0))
    return pl.pallas_call(
        functools.partial(_mixer_a_sample_kernel, v0=v0, width=width),
        grid=(hist // STATE_ROWS,),
        in_specs=[_resident((n, d)), blocks, nxt, _resident(vecs.shape), _resident(w1.shape),
                  _resident(w2.shape)],
        out_specs=[pl.BlockSpec((n, d), lambda k: (0, 0)), blocks],
        out_shape=[jax.ShapeDtypeStruct((n, d), jnp.float32),
                   jax.ShapeDtypeStruct((hist, n, d), jnp.float32)],
        scratch_shapes=[pltpu.VMEM((n, d), jnp.float32)],
        compiler_params=_params(1),
        name="mixer_a_sample",
    )(x, state, state, vecs, w1, w2)


def _mixer_b_sample_kernel(x_ref, st_ref, vec_ref, win_ref, wout_ref, o_ref, ns_ref, *, v0):
    d = x_ref.shape[-1]
    hist = st_ref.shape[0]
    tap = lambda k: vec_ref[v0 + 1 + k:v0 + 2 + k, :]
    x = x_ref[...]
    hn = _rms(x, vec_ref[v0:v0 + 1, :]).astype(jnp.bfloat16)
    b_gate = _dot(hn, win_ref[:, 0:d])
    c_gate = _dot(hn, win_ref[:, d:2 * d])
    h = _dot(hn, win_ref[:, 2 * d:3 * d])
    u = c_gate * h
    conv = st_ref[0] * tap(0)
    for k in range(1, hist):
        conv = conv + st_ref[k] * tap(k)
    conv = conv + u * tap(hist)
    ns_ref[0:hist - 1] = st_ref[1:hist]
    ns_ref[hist - 1] = u
    y = (b_gate * conv).astype(jnp.bfloat16)
    o_ref[...] = x + _dot(y, wout_ref[...])


def _mixer_b_sample(x, state, vecs, win, wout, *, width, v0):
    n, d = x.shape
    return pl.pallas_call(
        functools.partial(_mixer_b_sample_kernel, v0=v0),
        out_shape=[jax.ShapeDtypeStruct((n, d), jnp.float32),
                   jax.ShapeDtypeStruct((width - 1, n, d), jnp.float32)],
        compiler_params=pltpu.CompilerParams(vmem_limit_bytes=VMEM_LIMIT_BYTES),
        name="mixer_b_sample",
    )(x, state, vecs, win, wout)


def _pack_vectors(d, groups):
    rows, starts, n = [], [], 0
    for v in groups:
        v = v.reshape(-1, d)
        starts.append(n)
        rows.append(v)
        n += v.shape[0]
    fill = -n % SUBLANES
    if fill:
        rows.append(jnp.zeros((fill, d), jnp.float32))
    return jnp.concatenate(rows, axis=0), starts


def _cat(per_layer):
    return per_layer[0] if len(per_layer) == 1 else jnp.concatenate(per_layer, axis=0)


def kernel(x_prompt, x_sample, state_conv_a, state_conv_b, ffn1_norm, ffn1_w_gate, ffn1_w_up, ffn1_w_down, mix_norm, ffn2_norm, ffn2_w_gate, ffn2_w_up, ffn2_w_down, final_norm, a_w_pw1, a_b_pw1, a_w_dw, a_b_dw, a_ln_g, a_ln_b, a_w_pw2, a_b_pw2, b_w_in, b_w_conv, b_w_out):
    depth = ffn1_norm.shape[0]
    bsz, seq, d = x_prompt.shape
    n_dec = x_sample.shape[0]
    width_a = a_w_dw.shape[1]
    width_b = b_w_conv.shape[1]

    groups = [ffn1_norm, ffn2_norm, final_norm]
    for i in range(depth):
        j = i // 2
        if i % 2 == 0:
            groups += [mix_norm[i], a_b_pw1[j], a_b_dw[j], a_ln_g[j], a_ln_b[j], a_b_pw2[j], a_w_dw[j]]
        else:
            groups += [mix_norm[i], b_w_conv[j]]
    vecs, starts = _pack_vectors(d, groups)
    ffn1_row, ffn2_row, final_row = starts[0], starts[1], starts[2]
    mix_rows, p = [], 3
    for i in range(depth):
        mix_rows.append(starts[p])
        p += 7 if i % 2 == 0 else 2

    def ffn1_src(i):
        return [(ffn1_w_gate, i), (ffn1_w_up, i), (ffn1_w_down, i)]

    def ffn2_src(i):
        return [(ffn2_w_gate, i), (ffn2_w_up, i), (ffn2_w_down, i)]

    stage_src = []
    for i in range(depth):
        j = i // 2
        if i % 2 == 0:
            stage_src += [ffn1_src(i) + [(a_w_pw1, j)], [(a_w_pw2, j)] + ffn2_src(i)]
        else:
            stage_src += [ffn1_src(i) + [(b_w_in, j), (b_w_out, j)], ffn2_src(i)]
    stage_src.append([])
    ready = list(_cast_weights(stage_src[0]))

    xp = x_prompt.reshape(bsz * seq, d)
    xs = x_sample.reshape(n_dec, d)
    new_a_p, new_a_s, new_b_p, new_b_s = [], [], [], []
    for i in range(depth):
        j = i // 2
        fin = final_row if i == depth - 1 else None
        nxt1, nxt2 = stage_src[2 * i + 1], stage_src[2 * i + 2]
        if i % 2 == 0:
            kw = dict(v0=mix_rows[i], width=width_a)
            wg, wu, wd, w1 = ready
            x1, v, nb, xs, ready = _ffn_glu(xp, xs, vecs, wg, wu, wd, w1, g_row=ffn1_row + i,
                                            seq=seq, cast=nxt1, **kw)
            w2, wg, wu, wd = ready
            state = jnp.transpose(state_conv_a[j], (1, 0, 2))
            xs, ns = _mixer_a_sample(xs, state, vecs, w1, w2, **kw)
            xp, xs, ready = _conv_ffn(x1, v, xs, vecs, w2, wg, wu, wd, g_row=ffn2_row + i,
                                      final_row=fin, seq=seq, cast=nxt2, **kw)
            new_a_p.append(nb)
            new_a_s.append(jnp.transpose(ns, (1, 0, 2))[None])
        else:
            kw = dict(v0=mix_rows[i], width=width_b)
            wg, wu, wd, win, wout = ready
            xp, nb, xs, ready = _ffn_mixer_b(xp, xs, vecs, wg, wu, wd, win, wout,
                                             g_row=ffn1_row + i, seq=seq, cast=nxt1, **kw)
            wg, wu, wd = ready
            state = jnp.transpose(state_conv_b[j], (1, 0, 2))
            xs, ns = _mixer_b_sample(xs, state, vecs, win, wout, **kw)
            xp, xs, ready = _ffn(xp, xs, vecs, wg, wu, wd, g_row=ffn2_row + i, final_row=fin,
                                 cast=nxt2, name=f"ffn2_{i}")
            new_b_p.append(nb)
            new_b_s.append(jnp.transpose(ns, (1, 0, 2))[None])

    return (xp.reshape(bsz, seq, d), xs.reshape(n_dec, 1, d),
            _cat(new_a_p), _cat(new_a_s), _cat(new_b_p), _cat(new_b_s))
```

```python
import functools

import jax
import jax.numpy as jnp
from jax.experimental import pallas as pl
from jax.experimental.pallas import tpu as pltpu

RMS_EPS = 1e-6
LN_EPS = 1e-5
FFN_RES_WEIGHT = 0.5

SUBLANES = 8
LANES = 128
BF16_ROWS = 16
VMEM_LIMIT_BYTES = 56 * 1024 * 1024

TOKEN_TILE = 512
FF_CHUNK = 512
STATE_ROWS = 6
CAST_STEPS = 16


def _rms(x, g):
    return x * jax.lax.rsqrt(jnp.mean(x * x, axis=-1, keepdims=True) + RMS_EPS) * g


def _layer_norm(x, g, b):
    mu = jnp.mean(x, axis=-1, keepdims=True)
    xc = x - mu
    var = jnp.mean(xc * xc, axis=-1, keepdims=True)
    return xc * jax.lax.rsqrt(var + LN_EPS) * g + b


def _dot(a, b):
    return jnp.dot(a, b, preferred_element_type=jnp.float32)


def _resident(shape):
    zeros = (0,) * len(shape)
    return pl.BlockSpec(shape, lambda *_: zeros, pipeline_mode=pl.Buffered(1))


def _params(n_grid):
    return pltpu.CompilerParams(
        dimension_semantics=("arbitrary",) * n_grid,
        vmem_limit_bytes=VMEM_LIMIT_BYTES,
    )


def _cast_kernel(*refs):
    n = len(refs) // 2
    for src, dst in zip(refs[:n], refs[n:]):
        dst[...] = src[...].astype(jnp.bfloat16)


def _cast_weights(items):
    in_specs, out_specs, out_shape = [], [], []
    for w, layer in items:
        _, rows, cols = w.shape
        assert rows % (CAST_STEPS * BF16_ROWS) == 0
        blk = rows // CAST_STEPS
        in_specs.append(pl.BlockSpec((None, blk, cols), lambda i, layer=layer: (layer, i, 0)))
        out_specs.append(pl.BlockSpec((blk, cols), lambda i: (i, 0)))
        out_shape.append(jax.ShapeDtypeStruct((rows, cols), jnp.bfloat16))
    return pl.pallas_call(
        _cast_kernel,
        grid=(CAST_STEPS,),
        in_specs=in_specs,
        out_specs=out_specs,
        out_shape=out_shape,
        compiler_params=_params(1),
        name="cast_weights",
    )(*[w for w, _ in items])


def _side_cast_specs(items, n_tiles, tile_of_step):
    in_specs, out_specs, out_shape = [], [], []
    for w, layer in items:
        _, rows, cols = w.shape
        n_blocks = n_tiles
        while rows % (n_blocks * BF16_ROWS) or n_tiles % n_blocks:
            n_blocks -= 1
        blk, per = rows // n_blocks, n_tiles // n_blocks
        in_specs.append(pl.BlockSpec(
            (None, blk, cols), lambda s, layer=layer, per=per: (layer, tile_of_step(s) // per, 0)))
        out_specs.append(pl.BlockSpec((blk, cols), lambda s, per=per: (tile_of_step(s) // per, 0)))
        out_shape.append(jax.ShapeDtypeStruct((rows, cols), jnp.bfloat16))
    return in_specs, out_specs, out_shape


def _side_cast(srcs, dsts):
    for src, dst in zip(srcs, dsts):
        dst[...] = src[...].astype(jnp.bfloat16)


def _ffn_body(x, g, wg_ref, wu_ref, wd_ref):
    d_ff = wg_ref.shape[1]
    h = _rms(x, g).astype(jnp.bfloat16)
    y = None
    for c0 in range(0, d_ff, FF_CHUNK):
        c1 = min(c0 + FF_CHUNK, d_ff)
        gate = _dot(h, wg_ref[:, c0:c1])
        up = _dot(h, wu_ref[:, c0:c1])
        a = (gate * jax.nn.sigmoid(gate) * up).astype(jnp.bfloat16)
        part = _dot(a, wd_ref[c0:c1, :])
        y = part if y is None else y + part
    return x + FFN_RES_WEIGHT * y


def _ffn_out(x, vec_ref, wg_ref, wu_ref, wd_ref, g_row, final_row):
    out = _ffn_body(x, vec_ref[g_row:g_row + 1, :], wg_ref, wu_ref, wd_ref)
    if final_row is not None:
        out = _rms(out, vec_ref[final_row:final_row + 1, :])
    return out


def _ffn_kernel(x_ref, vec_ref, wg_ref, wu_ref, wd_ref, xs_ref, *rest, g_row, final_row, n_tiles):
    n_cast = (len(rest) - 2) // 2
    srcs, (o_ref, ys_ref), dsts = rest[:n_cast], rest[n_cast:n_cast + 2], rest[n_cast + 2:]
    s = pl.program_id(0)
    ffn = functools.partial(_ffn_out, vec_ref=vec_ref, wg_ref=wg_ref, wu_ref=wu_ref, wd_ref=wd_ref,
                            g_row=g_row, final_row=final_row)

    @pl.when(s < n_tiles)
    def _():
        o_ref[...] = ffn(x_ref[...])
        _side_cast(srcs, dsts)

    @pl.when(s == n_tiles)
    def _():
        ys_ref[...] = ffn(xs_ref[...])


def _ffn(x, xs, vecs, wg, wu, wd, *, g_row, final_row=None, cast=(), name):
    m, d = x.shape
    tm = min(TOKEN_TILE, m)
    n_tiles = m // tm
    tile = lambda s: jnp.minimum(s, n_tiles - 1)
    row = pl.BlockSpec((tm, d), lambda s: (tile(s), 0))
    c_in, c_out, c_shape = _side_cast_specs(cast, n_tiles, tile)
    res = pl.pallas_call(
        functools.partial(_ffn_kernel, g_row=g_row, final_row=final_row, n_tiles=n_tiles),
        grid=(n_tiles + 1,),
        in_specs=[row, _resident(vecs.shape), _resident(wg.shape), _resident(wu.shape),
                  _resident(wd.shape), _resident(xs.shape)] + c_in,
        out_specs=[row, _resident(xs.shape)] + c_out,
        out_shape=[jax.ShapeDtypeStruct((m, d), jnp.float32),
                   jax.ShapeDtypeStruct(xs.shape, jnp.float32)] + c_shape,
        compiler_params=_params(1),
        name=name,
    )(x, vecs, wg, wu, wd, xs, *[w for w, _ in cast])
    return res[0], res[1], res[2:]


CHAIN_TILES = 8
CHAIN_XLU_HOPS = 1
LN_ROWS = 64


def _zero_after(tiles, xlu_hops):
    b = jax.lax.bitcast_convert_type(tiles, jnp.uint32)
    t = b[0]
    for i in range(1, b.shape[0]):
        t = t | b[i]
    t = (t >> 16) >> 16
    for _ in range(xlu_hops):
        t = pltpu.roll(t, 1, axis=1)
    return t


def _after(x, zero):
    if zero is None:
        return x
    return jnp.where(zero == 0, x, 0.0)


def _conv_pad(width):
    return -(-(width - 1) // SUBLANES) * SUBLANES


def _dwconv(tail_ref, keep_tail, v_ref, w_ref, w_row, width, out_ref, *, chained):
    tm, d = v_ref.shape
    pad = tail_ref.shape[0]
    first = pad - (width - 1)
    g = CHAIN_TILES
    nq = (first + width - 1) // SUBLANES + 1
    assert pad == SUBLANES * (nq - 1) and (tm // SUBLANES) % g == 0
    sub = jax.lax.broadcasted_iota(jnp.int32, (g, SUBLANES, LANES), 1)
    taps_of = [[(q, SUBLANES * q + r - first) for q in range(nq)
                if 0 <= SUBLANES * q + r - first < width] for r in range(SUBLANES)]
    zero = None
    for c0 in range(0, d, LANES):
        lanes = slice(c0, c0 + LANES)
        w = [jnp.broadcast_to(w_ref[w_row + k:w_row + k + 1, lanes], (SUBLANES, LANES))
             for k in range(width)]
        win = tail_ref[:, lanes]
        if keep_tail is not None:
            win = jnp.where(keep_tail, win, 0.0)
        win = win.reshape(nq - 1, SUBLANES, LANES)
        halo = None
        for j0 in range(0, tm // SUBLANES, g):
            new = v_ref[j0 * SUBLANES:(j0 + g) * SUBLANES, lanes].reshape(g, SUBLANES, LANES)
            if chained:
                new = jnp.concatenate([new[:g - 1], _after(new[g - 1], zero)[None]], axis=0)
            x = jnp.concatenate([win, new], axis=0)
            acc, new_halo = None, []
            for r in range(SUBLANES):
                if not taps_of[r]:
                    continue
                lo = 0 if (r == 0 or halo is None) else 1
                n = g if r == 0 else g + 1
                y = None
                for q, k in taps_of[r]:
                    term = x[q + lo:q + n] * w[k][None]
                    y = term if y is None else y + term
                if r == 0:
                    z = y
                else:
                    rot = pltpu.roll(y, SUBLANES - r, axis=1)
                    if halo is not None:
                        rot = jnp.concatenate([halo[len(new_halo)], rot], axis=0)
                    new_halo.append(rot[g:g + 1])
                    z = jnp.where(sub < SUBLANES - r, rot[0:g], rot[1:g + 1])
                acc = z if acc is None else acc + z
            out_ref[j0 * SUBLANES:(j0 + g) * SUBLANES, lanes] = acc.reshape(g * SUBLANES, LANES)
            halo = new_halo
            win = x[g:g + nq - 1]
            if chained:
                zero = _zero_after(acc, CHAIN_XLU_HOPS)
    return zero


def _ffn_glu_kernel(x_ref, vec_ref, wg_ref, wu_ref, wd_ref, w1_ref, xs_ref, *rest,
                    g_row, v0, width, tiles_per_seq, n_tiles):
    n_cast = (len(rest) - 4) // 2
    srcs, dsts = rest[:n_cast], rest[n_cast + 4:]
    x1_ref, v_ref, nb_ref, ys_ref = rest[n_cast:n_cast + 4]
    s = pl.program_id(0)
    tm, d = x_ref.shape
    vec = lambda i: vec_ref[v0 + i:v0 + i + 1, :]
    ffn = functools.partial(_ffn_out, vec_ref=vec_ref, wg_ref=wg_ref, wu_ref=wu_ref, wd_ref=wd_ref,
                            g_row=g_row, final_row=None)

    @pl.when(s < n_tiles)
    def _():
        x1 = ffn(x_ref[...])
        x1_ref[...] = x1
        hn = _rms(x1, vec(0)).astype(jnp.bfloat16)
        a = _dot(hn, w1_ref[:, 0:d]) + vec(1)
        g = _dot(hn, w1_ref[:, d:2 * d]) + vec(2)
        v_ref[...] = a * jax.nn.sigmoid(g)
        _side_cast(srcs, dsts)

        @pl.when(s % tiles_per_seq == tiles_per_seq - 1)
        def _():
            nb_ref[0, 0] = v_ref[tm - (width - 1):tm, :]

    @pl.when(s == n_tiles)
    def _():
        ys_ref[...] = ffn(xs_ref[...])


def _ffn_glu(x, xs, vecs, wg, wu, wd, w1, *, g_row, v0, width, seq, cast=()):
    m, d = x.shape
    tm = min(TOKEN_TILE, seq)
    n_tiles, tiles_per_seq = m // tm, seq // tm
    tile = lambda s: jnp.minimum(s, n_tiles - 1)
    row = pl.BlockSpec((tm, d), lambda s: (tile(s), 0))
    c_in, c_out, c_shape = _side_cast_specs(cast, n_tiles, tile)
    res = pl.pallas_call(
        functools.partial(_ffn_glu_kernel, g_row=g_row, v0=v0, width=width,
                          tiles_per_seq=tiles_per_seq, n_tiles=n_tiles),
        grid=(n_tiles + 1,),
        in_specs=[row, _resident(vecs.shape), _resident(wg.shape), _resident(wu.shape),
                  _resident(wd.shape), _resident(w1.shape), _resident(xs.shape)] + c_in,
        out_specs=[row, row,
                   pl.BlockSpec((1, 1, width - 1, d),
                                lambda s: (0, tile(s) // tiles_per_seq, 0, 0)),
                   _resident(xs.shape)] + c_out,
        out_shape=[jax.ShapeDtypeStruct((m, d), jnp.float32),
                   jax.ShapeDtypeStruct((m, d), jnp.float32),
                   jax.ShapeDtypeStruct((1, m // seq, width - 1, d), jnp.float32),
                   jax.ShapeDtypeStruct(xs.shape, jnp.float32)] + c_shape,
        compiler_params=_params(1),
        name="ffn1_glu",
    )(x, vecs, wg, wu, wd, w1, xs, *[w for w, _ in cast])
    return res[0], res[1], res[2], res[3], res[4:]


def _conv_ln_swish(tail_ref, keep_tail, v_ref, vec_ref, v0, width, conv_ref, c_ref, *, chained):
    tm, d = v_ref.shape
    pad = tail_ref.shape[0]
    vec = lambda i: vec_ref[v0 + i:v0 + i + 1, :]
    zero = _dwconv(tail_ref, keep_tail, v_ref, vec_ref, v0 + 7, width, conv_ref, chained=chained)
    for r0 in range(0, tm, LN_ROWS):
        cin = conv_ref[r0:r0 + LN_ROWS, :]
        if chained:
            corner = _after(cin[0:SUBLANES, 0:LANES], zero)
            top = jnp.concatenate([corner, cin[0:SUBLANES, LANES:]], axis=1)
            cin = jnp.concatenate([top, cin[SUBLANES:]], axis=0)
        c = _layer_norm(cin + vec(3), vec(4), vec(5))
        c = c * jax.nn.sigmoid(c)
        c_ref[r0:r0 + LN_ROWS, :] = c.astype(jnp.bfloat16)
        if chained:
            zero = _zero_after(c.reshape(LN_ROWS // SUBLANES, SUBLANES, d)[:, :, 0:LANES], 0)
    tail_ref[...] = v_ref[tm - pad:tm, :]


def _conv_ffn_kernel(x1_ref, v_ref, vec_ref, w2_ref, wg_ref, wu_ref, wd_ref, xs_ref, *rest,
                     g_row, final_row, v0, width, tiles_per_seq):
    n_cast = (len(rest) - 5) // 2
    srcs, (o_ref, ys_ref), dsts = rest[:n_cast], rest[n_cast:n_cast + 2], rest[n_cast + 2:-3]
    c_buf, tail_ref, conv_ref = rest[-3:]
    s = pl.program_id(0)
    slot = s % 2
    stage = functools.partial(_conv_ln_swish, tail_ref, v_ref=v_ref, vec_ref=vec_ref, v0=v0,
                              width=width, conv_ref=conv_ref)
    ffn = functools.partial(_ffn_out, vec_ref=vec_ref, wg_ref=wg_ref, wu_ref=wu_ref, wd_ref=wd_ref,
                            g_row=g_row, final_row=final_row)

    @pl.when(s == 0)
    def _():
        tail_ref[...] = jnp.zeros_like(tail_ref)
        stage(keep_tail=None, c_ref=c_buf.at[0], chained=False)
        ys_ref[...] = ffn(xs_ref[...])

    @pl.when(s > 0)
    def _():
        c = c_buf[1 - slot]
        x2 = x1_ref[...] + _dot(c, w2_ref[...]) + vec_ref[v0 + 6:v0 + 7, :]
        o_ref[...] = ffn(x2)
        _side_cast(srcs, dsts)
        stage(keep_tail=s % tiles_per_seq != 0, c_ref=c_buf.at[slot], chained=True)


def _conv_ffn(x1, v, xs, vecs, w2, wg, wu, wd, *, g_row, final_row, v0, width, seq, cast=()):
    m, d = x1.shape
    tm = min(TOKEN_TILE, seq)
    n_tiles = m // tm
    pad = _conv_pad(width)
    tile = lambda s: jnp.maximum(s - 1, 0)
    behind = pl.BlockSpec((tm, d), lambda s: (tile(s), 0))
    ahead = pl.BlockSpec((tm, d), lambda s: (jnp.minimum(s, n_tiles - 1), 0))
    c_in, c_out, c_shape = _side_cast_specs(cast, n_tiles, tile)
    res = pl.pallas_call(
        functools.partial(_conv_ffn_kernel, g_row=g_row, final_row=final_row, v0=v0, width=width,
                          tiles_per_seq=seq // tm),
        grid=(n_tiles + 1,),
        in_specs=[behind, ahead, _resident(vecs.shape), _resident(w2.shape), _resident(wg.shape),
                  _resident(wu.shape), _resident(wd.shape), _resident(xs.shape)] + c_in,
        out_specs=[behind, _resident(xs.shape)] + c_out,
        out_shape=[jax.ShapeDtypeStruct((m, d), jnp.float32),
                   jax.ShapeDtypeStruct(xs.shape, jnp.float32)] + c_shape,
        scratch_shapes=[pltpu.VMEM((2, tm, d), jnp.bfloat16), pltpu.VMEM((pad, d), jnp.float32),
                        pltpu.VMEM((tm, d), jnp.float32)],
        compiler_params=_params(1),
        name="conv_ffn2",
    )(x1, v, vecs, w2, wg, wu, wd, xs, *[w for w, _ in cast])
    return res[0], res[1], res[2:]


def _ffn_mixer_b_kernel(x_ref, vec_ref, wg_ref, wu_ref, wd_ref, win_ref, wout_ref, xs_ref, *rest,
                        g_row, v0, width, tiles_per_seq, n_tiles):
    n_cast = (len(rest) - 6) // 2
    srcs, dsts = rest[:n_cast], rest[n_cast + 3:-3]
    o_ref, nb_ref, ys_ref = rest[n_cast:n_cast + 3]
    u_ref, tail_ref, conv_ref = rest[-3:]
    s = pl.program_id(0)
    tm, d = x_ref.shape
    pad = tail_ref.shape[0]
    ffn = functools.partial(_ffn_out, vec_ref=vec_ref, wg_ref=wg_ref, wu_ref=wu_ref, wd_ref=wd_ref,
                            g_row=g_row, final_row=None)

    @pl.when(s == 0)
    def _():
        tail_ref[...] = jnp.zeros_like(tail_ref)

    @pl.when(s < n_tiles)
    def _():
        x1 = ffn(x_ref[...])
        hn = _rms(x1, vec_ref[v0:v0 + 1, :]).astype(jnp.bfloat16)
        b_gate = _dot(hn, win_ref[:, 0:d])
        c_gate = _dot(hn, win_ref[:, d:2 * d])
        h = _dot(hn, win_ref[:, 2 * d:3 * d])
        u_ref[...] = c_gate * h
        _dwconv(tail_ref, s % tiles_per_seq != 0, u_ref, vec_ref, v0 + 1, width, conv_ref,
                chained=False)
        y = (b_gate * conv_ref[...]).astype(jnp.bfloat16)
        o_ref[...] = x1 + _dot(y, wout_ref[...])
        tail_ref[...] = u_ref[tm - pad:tm, :]
        _side_cast(srcs, dsts)

        @pl.when(s % tiles_per_seq == tiles_per_seq - 1)
        def _():
            nb_ref[0, 0] = u_ref[tm - (width - 1):tm, :]

    @pl.when(s == n_tiles)
    def _():
        ys_ref[...] = ffn(xs_ref[...])


def _ffn_mixer_b(x, xs, vecs, wg, wu, wd, win, wout, *, g_row, v0, width, seq, cast=()):
    m, d = x.shape
    tm = min(TOKEN_TILE, seq)
    n_tiles, tiles_per_seq = m // tm, seq // tm
    pad = _conv_pad(width)
    tile = lambda s: jnp.minimum(s, n_tiles - 1)
    row = pl.BlockSpec((tm, d), lambda s: (tile(s), 0))
    c_in, c_out, c_shape = _side_cast_specs(cast, n_tiles, tile)
    res = pl.pallas_call(
        functools.partial(_ffn_mixer_b_kernel, g_row=g_row, v0=v0, width=width,
                          tiles_per_seq=tiles_per_seq, n_tiles=n_tiles),
        grid=(n_tiles + 1,),
        in_specs=[row, _resident(vecs.shape), _resident(wg.shape), _resident(wu.shape),
                  _resident(wd.shape), _resident(win.shape), _resident(wout.shape),
                  _resident(xs.shape)] + c_in,
        out_specs=[row,
                   pl.BlockSpec((1, 1, width - 1, d),
                                lambda s: (0, tile(s) // tiles_per_seq, 0, 0)),
                   _resident(xs.shape)] + c_out,
        out_shape=[jax.ShapeDtypeStruct((m, d), jnp.float32),
                   jax.ShapeDtypeStruct((1, m // seq, width - 1, d), jnp.float32),
                   jax.ShapeDtypeStruct(xs.shape, jnp.float32)] + c_shape,
        scratch_shapes=[pltpu.VMEM((tm, d), jnp.float32), pltpu.VMEM((pad, d), jnp.float32),
                        pltpu.VMEM((tm, d), jnp.float32)],
        compiler_params=_params(1),
        name="ffn1_mixer_b",
    )(x, vecs, wg, wu, wd, win, wout, xs, *[w for w, _ in cast])
    return res[0], res[1], res[2], res[3:]


def _mixer_a_sample_kernel(x_ref, st_ref, nxt_ref, vec_ref, w1_ref, w2_ref, o_ref, ns_ref, acc_ref,
                           *, v0, width):
    k = pl.program_id(0)
    last = pl.num_programs(0) - 1
    d = x_ref.shape[-1]
    rows = st_ref.shape[0]
    vec = lambda i: vec_ref[v0 + i:v0 + i + 1, :]

    @pl.when(k == 0)
    def _():
        acc_ref[...] = jnp.zeros_like(acc_ref)

    part = None
    for i in range(rows):
        term = st_ref[i] * vec_ref[pl.ds(v0 + 7 + rows * k + i, 1), :]
        part = term if part is None else part + term
    acc_ref[...] += part
    ns_ref[0:rows - 1] = st_ref[1:rows]

    @pl.when(k < last)
    def _():
        ns_ref[rows - 1] = nxt_ref[...]

    @pl.when(k == last)
    def _():
        x = x_ref[...]
        hn = _rms(x, vec(0)).astype(jnp.bfloat16)
        a = _dot(hn, w1_ref[:, 0:d]) + vec(1)
        g = _dot(hn, w1_ref[:, d:2 * d]) + vec(2)
        v = a * jax.nn.sigmoid(g)
        ns_ref[rows - 1] = v
        c = _layer_norm(acc_ref[...] + v * vec(7 + width - 1) + vec(3), vec(4), vec(5))
        c = (c * jax.nn.sigmoid(c)).astype(jnp.bfloat16)
        o_ref[...] = x + _dot(c, w2_ref[...]) + vec(6)


def _mixer_a_sample(x, state, vecs, w1, w2, *, width, v0):
    n, d = x.shape
    hist = width - 1
    assert hist % STATE_ROWS == 0
    blocks = pl.BlockSpec((STATE_ROWS, n, d), lambda k: (k, 0, 0))
    nxt = pl.BlockSpec((None, n, d), lambda k: (jnp.minimum(STATE_ROWS * (k + 1), hist - 1), 0, 0))
    return pl.pallas_call(
        functools.partial(_mixer_a_sample_kernel, v0=v0, width=width),
        grid=(hist // STATE_ROWS,),
        in_specs=[_resident((n, d)), blocks, nxt, _resident(vecs.shape), _resident(w1.shape),
                  _resident(w2.shape)],
        out_specs=[pl.BlockSpec((n, d), lambda k: (0, 0)), blocks],
        out_shape=[jax.ShapeDtypeStruct((n, d), jnp.float32),
                   jax.ShapeDtypeStruct((hist, n, d), jnp.float32)],
        scratch_shapes=[pltpu.VMEM((n, d), jnp.float32)],
        compiler_params=_params(1),
        name="mixer_a_sample",
    )(x, state, state, vecs, w1, w2)


def _mixer_b_sample_kernel(x_ref, st_ref, vec_ref, win_ref, wout_ref, o_ref, ns_ref, *, v0):
    d = x_ref.shape[-1]
    hist = st_ref.shape[0]
    tap = lambda k: vec_ref[v0 + 1 + k:v0 + 2 + k, :]
    x = x_ref[...]
    hn = _rms(x, vec_ref[v0:v0 + 1, :]).astype(jnp.bfloat16)
    b_gate = _dot(hn, win_ref[:, 0:d])
    c_gate = _dot(hn, win_ref[:, d:2 * d])
    h = _dot(hn, win_ref[:, 2 * d:3 * d])
    u = c_gate * h
    conv = st_ref[0] * tap(0)
    for k in range(1, hist):
        conv = conv + st_ref[k] * tap(k)
    conv = conv + u * tap(hist)
    ns_ref[0:hist - 1] = st_ref[1:hist]
    ns_ref[hist - 1] = u
    y = (b_gate * conv).astype(jnp.bfloat16)
    o_ref[...] = x + _dot(y, wout_ref[...])


def _mixer_b_sample(x, state, vecs, win, wout, *, width, v0):
    n, d = x.shape
    return pl.pallas_call(
        functools.partial(_mixer_b_sample_kernel, v0=v0),
        out_shape=[jax.ShapeDtypeStruct((n, d), jnp.float32),
                   jax.ShapeDtypeStruct((width - 1, n, d), jnp.float32)],
        compiler_params=pltpu.CompilerParams(vmem_limit_bytes=VMEM_LIMIT_BYTES),
        name="mixer_b_sample",
    )(x, state, vecs, win, wout)


def _pack_vectors(d, groups):
    rows, starts, n = [], [], 0
    for v in groups:
        v = v.reshape(-1, d)
        starts.append(n)
        rows.append(v)
        n += v.shape[0]
    fill = -n % SUBLANES
    if fill:
        rows.append(jnp.zeros((fill, d), jnp.float32))
    return jnp.concatenate(rows, axis=0), starts


def _cat(per_layer):
    return per_layer[0] if len(per_layer) == 1 else jnp.concatenate(per_layer, axis=0)


def kernel(x_prompt, x_sample, state_conv_a, state_conv_b, ffn1_norm, ffn1_w_gate, ffn1_w_up, ffn1_w_down, mix_norm, ffn2_norm, ffn2_w_gate, ffn2_w_up, ffn2_w_down, final_norm, a_w_pw1, a_b_pw1, a_w_dw, a_b_dw, a_ln_g, a_ln_b, a_w_pw2, a_b_pw2, b_w_in, b_w_conv, b_w_out):
    depth = ffn1_norm.shape[0]
    bsz, seq, d = x_prompt.shape
    n_dec = x_sample.shape[0]
    width_a = a_w_dw.shape[1]
    width_b = b_w_conv.shape[1]

    groups = [ffn1_norm, ffn2_norm, final_norm]
    for i in range(depth):
        j = i // 2
        if i % 2 == 0:
            groups += [mix_norm[i], a_b_pw1[j], a_b_dw[j], a_ln_g[j], a_ln_b[j], a_b_pw2[j], a_w_dw[j]]
        else:
            groups += [mix_norm[i], b_w_conv[j]]
    vecs, starts = _pack_vectors(d, groups)
    ffn1_row, ffn2_row, final_row = starts[0], starts[1], starts[2]
    mix_rows, p = [], 3
    for i in range(depth):
        mix_rows.append(starts[p])
        p += 7 if i % 2 == 0 else 2

    def ffn1_src(i):
        return [(ffn1_w_gate, i), (ffn1_w_up, i), (ffn1_w_down, i)]

    def ffn2_src(i):
        return [(ffn2_w_gate, i), (ffn2_w_up, i), (ffn2_w_down, i)]

    stage_src = []
    for i in range(depth):
        j = i // 2
        if i % 2 == 0:
            stage_src += [ffn1_src(i) + [(a_w_pw1, j)], [(a_w_pw2, j)] + ffn2_src(i)]
        else:
            stage_src += [ffn1_src(i) + [(b_w_in, j), (b_w_out, j)], ffn2_src(i)]
    stage_src.append([])
    ready = list(_cast_weights(stage_src[0]))

    xp = x_prompt.reshape(bsz * seq, d)
    xs = x_sample.reshape(n_dec, d)
    new_a_p, new_a_s, new_b_p, new_b_s = [], [], [], []
    for i in range(depth):
        j = i // 2
        fin = final_row if i == depth - 1 else None
        nxt1, nxt2 = stage_src[2 * i + 1], stage_src[2 * i + 2]
        if i % 2 == 0:
            kw = dict(v0=mix_rows[i], width=width_a)
            wg, wu, wd, w1 = ready
            x1, v, nb, xs, ready = _ffn_glu(xp, xs, vecs, wg, wu, wd, w1, g_row=ffn1_row + i,
                                            seq=seq, cast=nxt1, **kw)
            w2, wg, wu, wd = ready
            state = jnp.transpose(state_conv_a[j], (1, 0, 2))
            xs, ns = _mixer_a_sample(xs, state, vecs, w1, w2, **kw)
            xp, xs, ready = _conv_ffn(x1, v, xs, vecs, w2, wg, wu, wd, g_row=ffn2_row + i,
                                      final_row=fin, seq=seq, cast=nxt2, **kw)
            new_a_p.append(nb)
            new_a_s.append(jnp.transpose(ns, (1, 0, 2))[None])
        else:
            kw = dict(v0=mix_rows[i], width=width_b)
            wg, wu, wd, win, wout = ready
            xp, nb, xs, ready = _ffn_mixer_b(xp, xs, vecs, wg, wu, wd, win, wout,
                                             g_row=ffn1_row + i, seq=seq, cast=nxt1, **kw)
            wg, wu, wd = ready
            state = jnp.transpose(state_conv_b[j], (1, 0, 2))
            xs, ns = _mixer_b_sample(xs, state, vecs, win, wout, **kw)
            xp, xs, ready = _ffn(xp, xs, vecs, wg, wu, wd, g_row=ffn2_row + i, final_row=fin,
                                 cast=nxt2, name=f"ffn2_{i}")
            new_b_p.append(nb)
            new_b_s.append(jnp.transpose(ns, (1, 0, 2))[None])

    return (xp.reshape(bsz, seq, d), xs.reshape(n_dec, 1, d),
            _cat(new_a_p), _cat(new_a_s), _cat(new_b_p), _cat(new_b_s))
```
